```python
import math
import jax, jax.numpy as jnp
from jax import lax
import numpy as np

D_MODEL = 4096
BATCH = 1
SEQ = 16384
DEPTH = 2

N_A = DEPTH // 2
N_B = DEPTH - N_A

POOL_WINDOWS = (2, 4, 8, 16)
N_POOL_GROUPS = len(POOL_WINDOWS)
POOL_GROUP_DIM = D_MODEL // N_POOL_GROUPS

QK_NOPE_DIM = 128
QK_ROPE_DIM = 64
V_HEAD_DIM = 128
N_HEADS = D_MODEL // V_HEAD_DIM
QK_HEAD_DIM = QK_NOPE_DIM + QK_ROPE_DIM
Q_LORA_RANK = 1024
KV_LORA_RANK = 512
ROPE_THETA = 10000.0
Q_BLOCK = 128

D_FF = 11008
CONV_WIDTH = 3

EPS = 1e-6

kernel_name = "yoco_pool_mla_convffn_trunk"


def rmsnorm(x, g):
    xf = x.astype(jnp.float32)
    y = xf * lax.rsqrt(jnp.mean(xf * xf, axis=-1, keepdims=True) + EPS)
    return (y * g.astype(jnp.float32)).astype(x.dtype)


def rope_tables(seq_len, dim, dtype):
    inv_freq = ROPE_THETA ** (-jnp.arange(0, dim, 2, dtype=jnp.float32) / dim)
    ang = jnp.arange(seq_len, dtype=jnp.float32)[:, None] * inv_freq[None, :]
    return jnp.cos(ang).astype(dtype), jnp.sin(ang).astype(dtype)


def apply_rope(x, cos, sin):
    x1, x2 = jnp.split(x, 2, axis=-1)
    return jnp.concatenate([x1 * cos - x2 * sin, x2 * cos + x1 * sin], axis=-1)


def multiscale_pool_mixer(xn, w_groups, scale):
    B, S, D = xn.shape
    xg = xn.astype(jnp.float32).reshape(B, S, N_POOL_GROUPS, POOL_GROUP_DIM)
    csum = jnp.concatenate(
        [jnp.zeros((B, 1, N_POOL_GROUPS, POOL_GROUP_DIM), jnp.float32), jnp.cumsum(xg, axis=1)],
        axis=1)
    t = jnp.arange(S)
    outs = []
    for gi, w in enumerate(POOL_WINDOWS):
        c = csum[:, :, gi]
        hi = c[:, 1:]
        lo = jnp.concatenate([jnp.zeros((B, w - 1, POOL_GROUP_DIM), jnp.float32),
                              c[:, :S - w + 1]], axis=1)
        count = jnp.minimum(t + 1, w).astype(jnp.float32)[None, :, None]
        outs.append((hi - lo) / count - xg[:, :, gi])
    pooled = jnp.stack(outs, axis=2).astype(xn.dtype)
    mixed = jnp.einsum('bsgc,gcd->bsgd', pooled, w_groups).reshape(B, S, D)
    return mixed * scale


def conv_ffn(xn, w_in, conv_w, conv_b, w_out):
    S = xn.shape[1]
    u = xn @ w_in
    up = jnp.pad(u, ((0, 0), (CONV_WIDTH - 1, 0), (0, 0)))
    uc = sum(conv_w[k] * up[:, k:k + S] for k in range(CONV_WIDTH)) + conv_b
    gate, val = jnp.split(uc, 2, axis=-1)
    return (jax.nn.silu(gate) * val) @ w_out


def shared_latent_kv(h, kv_norm, w_dkv, kv_lat_norm, w_ukv, cos, sin):
    hn = rmsnorm(h, kv_norm)
    ckv = hn @ w_dkv
    c_kv = rmsnorm(ckv[..., :KV_LORA_RANK], kv_lat_norm)
    k_rope = apply_rope(ckv[..., KV_LORA_RANK:], cos, sin)
    kv = jnp.einsum('bsc,chd->bshd', c_kv, w_ukv)
    return kv[..., :QK_NOPE_DIM], k_rope, kv[..., QK_NOPE_DIM:]


def mla_mixer(xn, w_dq, q_lat_norm, w_uq, w_o, k_nope, k_rope, v, cos, sin):
    B, S, _ = xn.shape
    c_q = rmsnorm(xn @ w_dq, q_lat_norm)
    q = jnp.einsum('bsc,chd->bshd', c_q, w_uq)
    q_nope = q[..., :QK_NOPE_DIM]
    q_rope = apply_rope(q[..., QK_NOPE_DIM:], cos[:, None, :], sin[:, None, :])
    nblk = S // Q_BLOCK
    qn_b = q_nope.reshape(B, nblk, Q_BLOCK, N_HEADS, QK_NOPE_DIM).transpose(1, 0, 2, 3, 4)
    qr_b = q_rope.reshape(B, nblk, Q_BLOCK, N_HEADS, QK_ROPE_DIM).transpose(1, 0, 2, 3, 4)
    kpos = jnp.arange(S)
    sm_scale = QK_HEAD_DIM ** -0.5

    def attend_block(args):
        qn, qr, bi = args
        s = (jnp.einsum('bqhd,bkhd->bhqk', qn, k_nope)
             + jnp.einsum('bqhr,bkr->bhqk', qr, k_rope))
        s = s.astype(jnp.float32) * sm_scale
        qpos = bi * Q_BLOCK + jnp.arange(Q_BLOCK)
        s = jnp.where(qpos[:, None] >= kpos[None, :], s, -jnp.inf)
        p = jax.nn.softmax(s, axis=-1).astype(v.dtype)
        return jnp.einsum('bhqk,bkhd->bqhd', p, v)

    o = lax.map(attend_block, (qn_b, qr_b, jnp.arange(nblk)))
    o = o.transpose(1, 0, 2, 3, 4).reshape(B, S, N_HEADS * V_HEAD_DIM)
    return o @ w_o


def setup_inputs(seed: int = 0) -> dict:
    key = jax.random.key(seed)
    ks = jax.random.split(key, 24)
    f32 = jnp.float32
    nrm = lambda k, shape, s: jax.random.normal(k, shape, f32) * s
    gain = lambda k, shape: 1.0 + 0.02 * jax.random.normal(k, shape, f32)
    return {
        "x": jax.random.normal(ks[0], (BATCH, SEQ, D_MODEL), f32),
        "a_norm": gain(ks[1], (N_A, D_MODEL)),
        "a_pool_w": nrm(ks[2], (N_A, N_POOL_GROUPS, POOL_GROUP_DIM, POOL_GROUP_DIM), POOL_GROUP_DIM ** -0.5),
        "a_scale": 1.0 + 0.1 * jax.random.normal(ks[3], (N_A, D_MODEL), f32),
        "kv_norm": gain(ks[4], (D_MODEL,)),
        "w_dkv": nrm(ks[5], (D_MODEL, KV_LORA_RANK + QK_ROPE_DIM), D_MODEL ** -0.5),
        "kv_lat_norm": gain(ks[6], (KV_LORA_RANK,)),
        "w_ukv": nrm(ks[7], (KV_LORA_RANK, N_HEADS, QK_NOPE_DIM + V_HEAD_DIM), KV_LORA_RANK ** -0.5),
        "b_norm": gain(ks[8], (N_B, D_MODEL)),
        "w_dq": nrm(ks[9], (N_B, D_MODEL, Q_LORA_RANK), D_MODEL ** -0.5),
        "q_lat_norm": gain(ks[10], (N_B, Q_LORA_RANK)),
        "w_uq": nrm(ks[11], (N_B, Q_LORA_RANK, N_HEADS, QK_HEAD_DIM), Q_LORA_RANK ** -0.5),
        "w_o": nrm(ks[12], (N_B, N_HEADS * V_HEAD_DIM, D_MODEL), (N_HEADS * V_HEAD_DIM) ** -0.5),
        "ffn_norm": gain(ks[13], (DEPTH, D_MODEL)),
        "ffn_w_in": nrm(ks[14], (DEPTH, D_MODEL, 2 * D_FF), D_MODEL ** -0.5),
        "ffn_conv_w": nrm(ks[15], (DEPTH, CONV_WIDTH, 2 * D_FF), CONV_WIDTH ** -0.5),
        "ffn_conv_b": nrm(ks[16], (DEPTH, 2 * D_FF), 0.01),
        "ffn_w_out": nrm(ks[17], (DEPTH, D_FF, D_MODEL), D_FF ** -0.5),
        "final_norm": gain(ks[18], (D_MODEL,)),
    }


def reference(x, a_norm, a_pool_w, a_scale, kv_norm, w_dkv, kv_lat_norm, w_ukv,
              b_norm, w_dq, q_lat_norm, w_uq, w_o, ffn_norm, ffn_w_in, ffn_conv_w,
              ffn_conv_b, ffn_w_out, final_norm):
    S = x.shape[1]
    cos, sin = rope_tables(S, QK_ROPE_DIM, x.dtype)
    h = x
    shared = None
    for layer in range(DEPTH):
        if layer < N_A:
            h = h + multiscale_pool_mixer(rmsnorm(h, a_norm[layer]), a_pool_w[layer], a_scale[layer])
        else:
            j = layer - N_A
            k_nope, k_rope, v = shared
            h = h + mla_mixer(rmsnorm(h, b_norm[j]), w_dq[j], q_lat_norm[j], w_uq[j], w_o[j],
                              k_nope, k_rope, v, cos, sin)
        h = h + conv_ffn(rmsnorm(h, ffn_norm[layer]), ffn_w_in[layer], ffn_conv_w[layer],
                         ffn_conv_b[layer], ffn_w_out[layer])
        if layer == N_A - 1:
            shared = shared_latent_kv(h, kv_norm, w_dkv, kv_lat_norm, w_ukv, cos, sin)
    return rmsnorm(h, final_norm)
```

```python
import functools
import math

import jax
import jax.numpy as jnp
from jax import lax
from jax.experimental import pallas as pl
from jax.experimental.pallas import tpu as pltpu

EPS = 1e-6
ROPE_THETA = 10000.0
POOL_WINDOWS = (2, 4, 8, 16)
CONV_WIDTH = 3
POOL_HALO = 16
CONV_HALO = 8
MASK_VALUE = -1e30
VMEM_LIMIT_BYTES = 56 * 1024 * 1024

F32 = jnp.float32
BF16 = jnp.bfloat16


def _rms(x, g):
    return x * lax.rsqrt(jnp.mean(x * x, axis=-1, keepdims=True) + EPS) * g


def _params(*sem):
    return pltpu.CompilerParams(dimension_semantics=sem, vmem_limit_bytes=VMEM_LIMIT_BYTES)


def _const_spec(shape):
    nd = len(shape)
    return pl.BlockSpec(shape, lambda *_: (0,) * nd, pipeline_mode=pl.Buffered(1))


def _pool_kernel(x_ref, halo_ref, g_ref, w_ref, sc_ref, o_ref, *, ts, gc):
    i = pl.program_id(0)
    g = g_ref[...]
    x = x_ref[...]
    xn = _rms(x, g)
    hn = jnp.where(i == 0, 0.0, _rms(halo_ref[...], g))
    pos = i * ts + lax.broadcasted_iota(jnp.int32, (ts, 1), 0)
    for gi, w in enumerate(POOL_WINDOWS):
        sl = slice(gi * gc, (gi + 1) * gc)
        xg = xn[:, sl]
        s = jnp.concatenate([hn[:, sl], xg], axis=0)
        k = 1
        while k < w:
            s = s + pltpu.roll(s, k, axis=0)
            k *= 2
        cnt = jnp.minimum(pos + 1, w).astype(F32)
        pooled = s[POOL_HALO:] / cnt - xg
        mixed = jnp.dot(pooled.astype(BF16), w_ref[gi], preferred_element_type=F32)
        o_ref[:, sl] = x[:, sl] + mixed * sc_ref[:, sl]


def _pool_layer(x, norm_g, w_bf16, scale, *, ts=256):
    s_len, d = x.shape
    ng, gc, _ = w_bf16.shape
    assert ng == len(POOL_WINDOWS) and ng * gc == d and s_len % ts == 0 and ts % POOL_HALO == 0
    per = ts // POOL_HALO
    return pl.pallas_call(
        functools.partial(_pool_kernel, ts=ts, gc=gc),
        grid=(s_len // ts,),
        in_specs=[
            pl.BlockSpec((ts, d), lambda i: (i, 0)),
            pl.BlockSpec((POOL_HALO, d), lambda i: (jnp.maximum(i * per - 1, 0), 0)),
            _const_spec((1, d)),
            _const_spec((ng, gc, gc)),
            _const_spec((1, d)),
        ],
        out_specs=pl.BlockSpec((ts, d), lambda i: (i, 0)),
        out_shape=jax.ShapeDtypeStruct((s_len, d), F32),
        compiler_params=_params("parallel"),
        name="pool_layer",
    )(x, x, norm_g.reshape(1, d), w_bf16, scale.reshape(1, d))


def _ffn_kernel(h_ref, g_ref, wg_ref, wv_ref, cwg_ref, cwv_ref, bg_ref, bv_ref, wo_ref, fg_ref,
                o_ref, xn_s, cg_s, cv_s, *, tm, tn, final):
    i = pl.program_id(0)
    f = pl.program_id(1)

    @pl.when(f == 0)
    def _():
        h = h_ref[...]
        xn_s[...] = _rms(h, g_ref[...]).astype(BF16)
        o_ref[...] = h

    @pl.when(i == 0)
    def _():
        cg_s[f] = jnp.zeros(cg_s.shape[1:], F32)
        cv_s[f] = jnp.zeros(cv_s.shape[1:], F32)

    xn = xn_s[...]

    def conv_branch(w_ref, cw_ref, b_ref, c_s):
        u = jnp.dot(xn, w_ref[...], preferred_element_type=F32)
        cat = jnp.concatenate([c_s[f], u], axis=0)
        u1 = pltpu.roll(cat, 1, axis=0)[CONV_HALO:]
        u2 = pltpu.roll(cat, 2, axis=0)[CONV_HALO:]
        c_s[f] = u[tm - CONV_HALO:]
        cw = cw_ref[...]
        return cw[0:1] * u2 + cw[1:2] * u1 + cw[2:3] * u + b_ref[...]

    gate = conv_branch(wg_ref, cwg_ref, bg_ref, cg_s)
    val = conv_branch(wv_ref, cwv_ref, bv_ref, cv_s)
    act = (gate * (1.0 / (1.0 + jnp.exp(-gate))) * val).astype(BF16)
    d = o_ref.shape[1]
    for n in range(d // tn):
        sl = slice(n * tn, (n + 1) * tn)
        o_ref[:, sl] += jnp.dot(act, wo_ref[:, sl], preferred_element_type=F32)

    if final:
        @pl.when(f == pl.num_programs(1) - 1)
        def _():
            o_ref[...] = _rms(o_ref[...], fg_ref[...])


def _ffn_layer(h, norm_g, w_in_bf16, conv_w, conv_b, w_out_bf16, final_g, *, final, tm=512, tf=256,
               tn=1024):
    s_len, d = h.shape
    dff = w_out_bf16.shape[0]
    tn = min(tn, d)
    assert s_len % tm == 0 and dff % tf == 0 and d % tn == 0 and w_in_bf16.shape == (d, 2 * dff)
    nf = dff // tf
    conv_b2 = conv_b.reshape(1, 2 * dff)
    return pl.pallas_call(
        functools.partial(_ffn_kernel, tm=tm, tn=tn, final=final),
        grid=(s_len // tm, nf),
        in_specs=[
            pl.BlockSpec((tm, d), lambda i, f: (i, 0), pipeline_mode=pl.Buffered(1)),
            _const_spec((1, d)),
            pl.BlockSpec((d, tf), lambda i, f: (0, f)),
            pl.BlockSpec((d, tf), lambda i, f: (0, nf + f)),
            pl.BlockSpec((CONV_WIDTH, tf), lambda i, f: (0, f)),
            pl.BlockSpec((CONV_WIDTH, tf), lambda i, f: (0, nf + f)),
            pl.BlockSpec((1, tf), lambda i, f: (0, f)),
            pl.BlockSpec((1, tf), lambda i, f: (0, nf + f)),
            pl.BlockSpec((tf, d), lambda i, f: (f, 0)),
            _const_spec((1, d)),
        ],
        out_specs=pl.BlockSpec((tm, d), lambda i, f: (i, 0)),
        out_shape=jax.ShapeDtypeStruct((s_len, d), F32),
        scratch_shapes=[
            pltpu.VMEM((tm, d), BF16),
            pltpu.VMEM((nf, CONV_HALO, tf), F32),
            pltpu.VMEM((nf, CONV_HALO, tf), F32),
        ],
        compiler_params=_params("arbitrary", "arbitrary"),
        name="conv_ffn",
    )(h, norm_g.reshape(1, d), w_in_bf16, w_in_bf16, conv_w, conv_w, conv_b2, conv_b2, w_out_bf16,
      final_g.reshape(1, d))


def _kv_kernel(h_ref, g_ref, wd_ref, lat_ref, cos_ref, sin_ref, wuk_ref, wuvt_ref,
               k_ref, vt_ref, c_s, ct_s, kr_s, *, lora, rope, nope, hg):
    gidx = pl.program_id(1)

    @pl.when(gidx == 0)
    def _():
        hn = _rms(h_ref[...], g_ref[...]).astype(BF16)
        ckv = jnp.dot(hn, wd_ref[...], preferred_element_type=F32)
        c = _rms(ckv[:, :lora], lat_ref[...])
        c_s[...] = c.astype(BF16)
        ct_s[...] = c.T.astype(BF16)
        kr = ckv[:, lora:lora + rope] * cos_ref[...] + ckv[:, lora + rope:] * sin_ref[...]
        kr_s[...] = kr.astype(BF16)

    kn = jnp.dot(c_s[...], wuk_ref[...], preferred_element_type=F32)
    vt = jnp.dot(wuvt_ref[...], ct_s[...], preferred_element_type=F32)
    vd = vt.shape[0] // hg
    for hh in range(hg):
        k_ref[hh, :, 0:nope] = kn[:, hh * nope:(hh + 1) * nope].astype(BF16)
        k_ref[hh, :, nope:nope + rope] = kr_s[...]
        vt_ref[hh, 0] = vt[hh * vd:(hh + 1) * vd].astype(BF16)


def _kv_proj(h, norm_g, wd_ext_bf16, lat_g, cos2, sin2, wuk_bf16, wuvt_bf16, *, n_heads, nope, rope,
             tm=512, hg=8):
    s_len, d = h.shape
    lora = wuk_bf16.shape[0]
    vd = wuvt_bf16.shape[0] // n_heads
    hg = min(hg, n_heads)
    assert s_len % tm == 0 and n_heads % hg == 0 and wd_ext_bf16.shape == (d, lora + 2 * rope)
    return pl.pallas_call(
        functools.partial(_kv_kernel, lora=lora, rope=rope, nope=nope, hg=hg),
        grid=(s_len // tm, n_heads // hg),
        in_specs=[
            pl.BlockSpec((tm, d), lambda i, g: (i, 0)),
            _const_spec((1, d)),
            _const_spec((d, lora + 2 * rope)),
            _const_spec((1, lora)),
            pl.BlockSpec((tm, rope), lambda i, g: (i, 0)),
            pl.BlockSpec((tm, rope), lambda i, g: (i, 0)),
            pl.BlockSpec((lora, hg * nope), lambda i, g: (0, g)),
            pl.BlockSpec((hg * vd, lora), lambda i, g: (g, 0)),
        ],
        out_specs=[
            pl.BlockSpec((hg, tm, nope + rope), lambda i, g: (g, i, 0)),
            pl.BlockSpec((hg, 1, vd, tm), lambda i, g: (g, i, 0, 0)),
        ],
        out_shape=[
            jax.ShapeDtypeStruct((n_heads, s_len, nope + rope), BF16),
            jax.ShapeDtypeStruct((n_heads, s_len // tm, vd, tm), BF16),
        ],
        scratch_shapes=[
            pltpu.VMEM((tm, lora), BF16),
            pltpu.VMEM((lora, tm), BF16),
            pltpu.VMEM((tm, rope), BF16),
        ],
        compiler_params=_params("parallel", "arbitrary"),
        name="kv_proj",
    )(h, norm_g.reshape(1, d), wd_ext_bf16, lat_g.reshape(1, lora), cos2, sin2, wuk_bf16, wuvt_bf16)


def _q_kernel(h_ref, g_ref, wdt_ref, lat_ref, cos_ref, sin_ref, wut_ref, qt_ref, cqt_s,
              *, nope, rope, hg):
    gidx = pl.program_id(1)

    @pl.when(gidx == 0)
    def _():
        xn = _rms(h_ref[...], g_ref[...]).astype(BF16)
        cqt = lax.dot_general(wdt_ref[...], xn, (((1,), (1,)), ((), ())),
                              preferred_element_type=F32)
        ms = jnp.mean(cqt * cqt, axis=0, keepdims=True)
        cqt_s[...] = (cqt * lax.rsqrt(ms + EPS) * lat_ref[...]).astype(BF16)

    qt = jnp.dot(wut_ref[...], cqt_s[...], preferred_element_type=F32)
    cos_t = cos_ref[...]
    sin_t = sin_ref[...]
    hd = nope + rope
    half = rope // 2
    for hh in range(hg):
        b = hh * hd
        qt_ref[hh, 0:nope] = qt[b:b + nope].astype(BF16)
        x1 = qt[b + nope:b + nope + half]
        x2 = qt[b + nope + half:b + hd]
        qt_ref[hh, nope:nope + half] = (x1 * cos_t - x2 * sin_t).astype(BF16)
        qt_ref[hh, nope + half:hd] = (x2 * cos_t + x1 * sin_t).astype(BF16)


def _q_proj(h, norm_g, wdqt_bf16, lat_g, cos_t, sin_t, wuqt_bf16, *, n_heads, nope, rope, tm=512, hg=8):
    s_len, d = h.shape
    qlora = wdqt_bf16.shape[0]
    hd = nope + rope
    hg = min(hg, n_heads)
    assert s_len % tm == 0 and n_heads % hg == 0 and wuqt_bf16.shape == (n_heads * hd, qlora)
    return pl.pallas_call(
        functools.partial(_q_kernel, nope=nope, rope=rope, hg=hg),
        grid=(s_len // tm, n_heads // hg),
        in_specs=[
            pl.BlockSpec((tm, d), lambda i, g: (i, 0)),
            _const_spec((1, d)),
            _const_spec((qlora, d)),
            _const_spec((qlora, 1)),
            pl.BlockSpec((rope // 2, tm), lambda i, g: (0, i)),
            pl.BlockSpec((rope // 2, tm), lambda i, g: (0, i)),
            pl.BlockSpec((hg * hd, qlora), lambda i, g: (g, 0)),
        ],
        out_specs=pl.BlockSpec((hg, hd, tm), lambda i, g: (g, 0, i)),
        out_shape=jax.ShapeDtypeStruct((n_heads, hd, s_len), BF16),
        scratch_shapes=[pltpu.VMEM((qlora, tm), BF16)],
        compiler_params=_params("parallel", "arbitrary"),
        name="q_proj",
    )(h, norm_g.reshape(1, d), wdqt_bf16, lat_g.reshape(qlora, 1), cos_t, sin_t, wuqt_bf16)


def _attn_kernel(qt_ref, k_ref, vt_ref, o_ref, *, t, scale_log2e):
    i = pl.program_id(1)
    qt = qt_ref[0]
    vd = vt_ref.shape[2]

    def step(j, carry, masked):
        m, l, acc = carry
        kblk = k_ref[0, pl.ds(pl.multiple_of(j * t, t), t), :]
        s = jnp.dot(kblk, qt, preferred_element_type=F32)
        if masked:
            kpos = lax.broadcasted_iota(jnp.int32, (t, t), 0)
            qpos = lax.broadcasted_iota(jnp.int32, (t, t), 1)
            s = jnp.where(kpos <= qpos, s, MASK_VALUE)
        m_new = jnp.maximum(m, jnp.max(s, axis=0, keepdims=True))
        alpha = jnp.exp2((m - m_new) * scale_log2e)
        p = jnp.exp2((s - m_new) * scale_log2e)
        l = alpha * l + jnp.sum(p, axis=0, keepdims=True)
        acc = alpha * acc + jnp.dot(vt_ref[0, j], p.astype(BF16), preferred_element_type=F32)
        return m_new, l, acc

    init = (jnp.full((1, t), MASK_VALUE, F32), jnp.zeros((1, t), F32), jnp.zeros((vd, t), F32))
    carry = lax.fori_loop(0, i, lambda j, c: step(j, c, False), init)
    m, l, acc = step(i, carry, True)
    o_ref[...] = (acc / l).T.astype(o_ref.dtype)


def _attention(qt, k, vt, *, sm_scale):
    n_heads, hd, s_len = qt.shape
    _, nk, vd, t = vt.shape
    assert k.shape == (n_heads, s_len, hd) and nk * t == s_len
    return pl.pallas_call(
        functools.partial(_attn_kernel, t=t, scale_log2e=sm_scale * math.log2(math.e)),
        grid=(n_heads, nk),
        in_specs=[
            pl.BlockSpec((1, hd, t), lambda h, i: (h, 0, i)),
            pl.BlockSpec((1, s_len, hd), lambda h, i: (h, 0, 0)),
            pl.BlockSpec((1, nk, vd, t), lambda h, i: (h, 0, 0, 0)),
        ],
        out_specs=pl.BlockSpec((t, vd), lambda h, i: (i, h)),
        out_shape=jax.ShapeDtypeStruct((s_len, n_heads * vd), BF16),
        compiler_params=_params("parallel", "parallel"),
        name="mla_attention",
    )(qt, k, vt)


def _oproj_kernel(a_ref, w_ref, r_ref, o_ref):
    o_ref[...] = r_ref[...] + jnp.dot(a_ref[...], w_ref[...], preferred_element_type=F32)


def _out_proj(a_bf16, w_bf16, resid, *, tm=512, tn=1024):
    s_len, kdim = a_bf16.shape
    d = w_bf16.shape[1]
    tn = min(tn, d)
    assert s_len % tm == 0 and d % tn == 0
    return pl.pallas_call(
        _oproj_kernel,
        grid=(s_len // tm, d // tn),
        in_specs=[
            pl.BlockSpec((tm, kdim), lambda i, j: (i, 0)),
            pl.BlockSpec((kdim, tn), lambda i, j: (0, j)),
            pl.BlockSpec((tm, tn), lambda i, j: (i, j)),
        ],
        out_specs=pl.BlockSpec((tm, tn), lambda i, j: (i, j)),
        out_shape=jax.ShapeDtypeStruct((s_len, d), F32),
        compiler_params=_params("parallel", "parallel"),
        name="attn_out_proj",
    )(a_bf16, w_bf16, resid)


def _trunk(x, a_norm, a_pool_w, a_scale, kv_norm, w_dkv, kv_lat_norm, w_ukv, b_norm, w_dq, q_lat_norm,
           w_uq, w_o, ffn_norm, ffn_w_in, ffn_conv_w, ffn_conv_b, ffn_w_out, final_norm,
           *, pool_ts=256, ffn_tm=512, ffn_tf=256, proj_tm=512, attn_t=512, oproj_tm=512):
    batch, s_len, d = x.shape
    depth = ffn_norm.shape[0]
    n_a = a_norm.shape[0]
    n_b = b_norm.shape[0]
    assert n_a + n_b == depth and n_a >= 1
    lora, n_heads, kvd = w_ukv.shape
    hd = w_uq.shape[-1]
    rope = w_dkv.shape[1] - lora
    nope = hd - rope
    vd = kvd - nope
    half = rope // 2

    inv_freq = ROPE_THETA ** (-jnp.arange(0, rope, 2, dtype=F32) / rope)
    ang = jnp.arange(s_len, dtype=F32)[:, None] * inv_freq[None, :]
    cos, sin = jnp.cos(ang), jnp.sin(ang)
    cos2 = jnp.concatenate([cos, cos], axis=1)
    sin2 = jnp.concatenate([-sin, sin], axis=1)
    cos_t, sin_t = cos.T, sin.T

    bf = lambda w: w.astype(BF16)
    wd_ext = bf(jnp.concatenate([w_dkv, w_dkv[:, lora + half:], w_dkv[:, lora:lora + half]], axis=1))
    wuk = bf(w_ukv[:, :, :nope].reshape(lora, n_heads * nope))
    wuvt = bf(w_ukv[:, :, nope:].reshape(lora, n_heads * vd).T)
    final_g = final_norm

    outs = []
    for b in range(batch):
        h = x[b]
        shared = None
        for layer in range(depth):
            last = layer == depth - 1
            if layer < n_a:
                h = _pool_layer(h, a_norm[layer], bf(a_pool_w[layer]), a_scale[layer], ts=pool_ts)
            else:
                j = layer - n_a
                qlora = w_dq.shape[2]
                qt = _q_proj(h, b_norm[j], bf(w_dq[j].T), q_lat_norm[j], cos_t, sin_t,
                             bf(w_uq[j].reshape(qlora, n_heads * hd).T),
                             n_heads=n_heads, nope=nope, rope=rope, tm=proj_tm)
                o = _attention(qt, shared[0], shared[1], sm_scale=hd ** -0.5)
                h = _out_proj(o, bf(w_o[j]), h, tm=oproj_tm)
            h = _ffn_layer(h, ffn_norm[layer], bf(ffn_w_in[layer]), ffn_conv_w[layer], ffn_conv_b[layer],
                           bf(ffn_w_out[layer]), final_g, final=last, tm=ffn_tm, tf=ffn_tf)
            if layer == n_a - 1:
                shared = _kv_proj(h, kv_norm, wd_ext, kv_lat_norm, cos2, sin2, wuk, wuvt,
                                  n_heads=n_heads, nope=nope, rope=rope, tm=attn_t)
        outs.append(h)
    return jnp.stack(outs, axis=0)


def kernel(x, a_norm, a_pool_w, a_scale, kv_norm, w_dkv, kv_lat_norm, w_ukv, b_norm, w_dq, q_lat_norm,
           w_uq, w_o, ffn_norm, ffn_w_in, ffn_conv_w, ffn_conv_b, ffn_w_out, final_norm):
    return _trunk(x, a_norm, a_pool_w, a_scale, kv_norm, w_dkv, kv_lat_norm, w_ukv, b_norm, w_dq,
                  q_lat_norm, w_uq, w_o, ffn_norm, ffn_w_in, ffn_conv_w, ffn_conv_b, ffn_w_out,
                  final_norm)
```

```python
import functools
import math

import jax
import jax.numpy as jnp
from jax import lax
from jax.experimental import pallas as pl
from jax.experimental.pallas import tpu as pltpu

EPS = 1e-6
ROPE_THETA = 10000.0
POOL_WINDOWS = (2, 4, 8, 16)
CONV_WIDTH = 3
POOL_HALO = 16
CONV_HALO = 8
MASK_VALUE = -1e30
VMEM_LIMIT_BYTES = 56 * 1024 * 1024
LANES = 128
BF16_SUBLANES = 16

F32 = jnp.float32
BF16 = jnp.bfloat16


def _rms(x, g):
    return x * lax.rsqrt(jnp.mean(x * x, axis=-1, keepdims=True) + EPS) * g


def _params(*sem):
    return pltpu.CompilerParams(dimension_semantics=sem, vmem_limit_bytes=VMEM_LIMIT_BYTES)


def _const_spec(shape):
    nd = len(shape)
    return pl.BlockSpec(shape, lambda *_: (0,) * nd, pipeline_mode=pl.Buffered(1))


def _pool_kernel(x_ref, halo_ref, g_ref, w_ref, sc_ref, o_ref, *, ts, gc):
    i = pl.program_id(0)
    g = g_ref[...]
    x = x_ref[...]
    xn = _rms(x, g)
    hn = jnp.where(i == 0, 0.0, _rms(halo_ref[...], g))
    pos = i * ts + lax.broadcasted_iota(jnp.int32, (ts, 1), 0)
    for gi, w in enumerate(POOL_WINDOWS):
        sl = slice(gi * gc, (gi + 1) * gc)
        xg = xn[:, sl]
        s = jnp.concatenate([hn[:, sl], xg], axis=0)
        k = 1
        while k < w:
            s = s + pltpu.roll(s, k, axis=0)
            k *= 2
        cnt = jnp.minimum(pos + 1, w).astype(F32)
        pooled = s[POOL_HALO:] / cnt - xg
        mixed = jnp.dot(pooled.astype(BF16), w_ref[gi], preferred_element_type=F32)
        o_ref[:, sl] = x[:, sl] + mixed * sc_ref[:, sl]


def _pool_layer(x, norm_g, w_bf16, scale, *, ts=256):
    s_len, d = x.shape
    ng, gc, _ = w_bf16.shape
    assert ng == len(POOL_WINDOWS) and ng * gc == d and s_len % ts == 0 and ts % POOL_HALO == 0
    per = ts // POOL_HALO
    return pl.pallas_call(
        functools.partial(_pool_kernel, ts=ts, gc=gc),
        grid=(s_len // ts,),
        in_specs=[
            pl.BlockSpec((ts, d), lambda i: (i, 0)),
            pl.BlockSpec((POOL_HALO, d), lambda i: (jnp.maximum(i * per - 1, 0), 0)),
            _const_spec((1, d)),
            _const_spec((ng, gc, gc)),
            _const_spec((1, d)),
        ],
        out_specs=pl.BlockSpec((ts, d), lambda i: (i, 0)),
        out_shape=jax.ShapeDtypeStruct((s_len, d), F32),
        compiler_params=_params("parallel"),
        name="pool_layer",
    )(x, x, norm_g.reshape(1, d), w_bf16, scale.reshape(1, d))


def _ffn_kernel(h_ref, g_ref, wg_ref, wv_ref, cwg_ref, cwv_ref, bg_ref, bv_ref, wo_ref, fg_ref,
                o_ref, xn_s, cg_s, cv_s, *, tm, tn, final):
    i = pl.program_id(0)
    f = pl.program_id(1)

    @pl.when(f == 0)
    def _():
        h = h_ref[...]
        xn_s[...] = _rms(h, g_ref[...]).astype(BF16)
        o_ref[...] = h

    @pl.when(i == 0)
    def _():
        cg_s[f] = jnp.zeros(cg_s.shape[1:], F32)
        cv_s[f] = jnp.zeros(cv_s.shape[1:], F32)

    xn = xn_s[...]

    def conv_branch(w_ref, cw_ref, b_ref, c_s):
        u = jnp.dot(xn, w_ref[...], preferred_element_type=F32)
        cat = jnp.concatenate([c_s[f], u], axis=0)
        u1 = pltpu.roll(cat, 1, axis=0)[CONV_HALO:]
        u2 = pltpu.roll(cat, 2, axis=0)[CONV_HALO:]
        c_s[f] = u[tm - CONV_HALO:]
        cw = cw_ref[...]
        return cw[0:1] * u2 + cw[1:2] * u1 + cw[2:3] * u + b_ref[...]

    gate = conv_branch(wg_ref, cwg_ref, bg_ref, cg_s)
    val = conv_branch(wv_ref, cwv_ref, bv_ref, cv_s)
    act = (gate * (1.0 / (1.0 + jnp.exp(-gate))) * val).astype(BF16)
    d = o_ref.shape[1]
    for n in range(d // tn):
        sl = slice(n * tn, (n + 1) * tn)
        o_ref[:, sl] += jnp.dot(act, wo_ref[:, sl], preferred_element_type=F32)

    if final:
        @pl.when(f == pl.num_programs(1) - 1)
        def _():
            o_ref[...] = _rms(o_ref[...], fg_ref[...])


def _ffn_layer(h, norm_g, w_in_bf16, conv_w, conv_b, w_out_bf16, final_g, *, final, tm=512, tf=256,
               tn=1024):
    s_len, d = h.shape
    dff = w_out_bf16.shape[0]
    tn = min(tn, d)
    assert s_len % tm == 0 and dff % tf == 0 and d % tn == 0 and w_in_bf16.shape == (d, 2 * dff)
    nf = dff // tf
    conv_b2 = conv_b.reshape(1, 2 * dff)
    return pl.pallas_call(
        functools.partial(_ffn_kernel, tm=tm, tn=tn, final=final),
        grid=(s_len // tm, nf),
        in_specs=[
            pl.BlockSpec((tm, d), lambda i, f: (i, 0), pipeline_mode=pl.Buffered(1)),
            _const_spec((1, d)),
            pl.BlockSpec((d, tf), lambda i, f: (0, f)),
            pl.BlockSpec((d, tf), lambda i, f: (0, nf + f)),
            pl.BlockSpec((CONV_WIDTH, tf), lambda i, f: (0, f)),
            pl.BlockSpec((CONV_WIDTH, tf), lambda i, f: (0, nf + f)),
            pl.BlockSpec((1, tf), lambda i, f: (0, f)),
            pl.BlockSpec((1, tf), lambda i, f: (0, nf + f)),
            pl.BlockSpec((tf, d), lambda i, f: (f, 0)),
            _const_spec((1, d)),
        ],
        out_specs=pl.BlockSpec((tm, d), lambda i, f: (i, 0)),
        out_shape=jax.ShapeDtypeStruct((s_len, d), F32),
        scratch_shapes=[
            pltpu.VMEM((tm, d), BF16),
            pltpu.VMEM((nf, CONV_HALO, tf), F32),
            pltpu.VMEM((nf, CONV_HALO, tf), F32),
        ],
        compiler_params=_params("arbitrary", "arbitrary"),
        name="conv_ffn",
    )(h, norm_g.reshape(1, d), w_in_bf16, w_in_bf16, conv_w, conv_w, conv_b2, conv_b2, w_out_bf16,
      final_g.reshape(1, d))


def _kv_kernel(h_ref, g_ref, wd_ref, lat_ref, cos_ref, sin_ref, wuk_ref, wuvt_ref,
               k_ref, vt_ref, c_s, ct_s, kr_s, *, lora, rope, nope, hg):
    gidx = pl.program_id(1)

    @pl.when(gidx == 0)
    def _():
        hn = _rms(h_ref[...], g_ref[...]).astype(BF16)
        ckv = jnp.dot(hn, wd_ref[...], preferred_element_type=F32)
        c = _rms(ckv[:, :lora], lat_ref[...])
        c_s[...] = c.astype(BF16)
        ct_s[...] = c.T.astype(BF16)
        kr = ckv[:, lora:lora + rope] * cos_ref[...] + ckv[:, lora + rope:] * sin_ref[...]
        kr_s[...] = kr.astype(BF16)

    kn = jnp.dot(c_s[...], wuk_ref[...], preferred_element_type=F32)
    vt = jnp.dot(wuvt_ref[...], ct_s[...], preferred_element_type=F32)
    vd = vt.shape[0] // hg
    tm = kn.shape[0]
    hd = nope + rope
    k_pad = k_ref.shape[2] - hd
    ones_rows = vt_ref.shape[1] - vd
    for hh in range(hg):
        k_ref[hh, :, 0:nope] = kn[:, hh * nope:(hh + 1) * nope].astype(BF16)
        k_ref[hh, :, nope:hd] = kr_s[...]
        if k_pad:
            k_ref[hh, :, hd:hd + k_pad] = jnp.zeros((tm, k_pad), BF16)
        vt_ref[hh, 0:vd] = vt[hh * vd:(hh + 1) * vd].astype(BF16)
        vt_ref[hh, vd:vd + ones_rows] = jnp.ones((ones_rows, tm), BF16)


def _kv_proj(h, norm_g, wd_ext_bf16, lat_g, cos2, sin2, wuk_bf16, wuvt_bf16, *, n_heads, nope, rope,
             hd_pad, ones_rows, tm=512, hg=8):
    s_len, d = h.shape
    lora = wuk_bf16.shape[0]
    vd = wuvt_bf16.shape[0] // n_heads
    hg = min(hg, n_heads)
    assert s_len % tm == 0 and n_heads % hg == 0 and wd_ext_bf16.shape == (d, lora + 2 * rope)
    return pl.pallas_call(
        functools.partial(_kv_kernel, lora=lora, rope=rope, nope=nope, hg=hg),
        grid=(s_len // tm, n_heads // hg),
        in_specs=[
            pl.BlockSpec((tm, d), lambda i, g: (i, 0)),
            _const_spec((1, d)),
            _const_spec((d, lora + 2 * rope)),
            _const_spec((1, lora)),
            pl.BlockSpec((tm, rope), lambda i, g: (i, 0)),
            pl.BlockSpec((tm, rope), lambda i, g: (i, 0)),
            pl.BlockSpec((lora, hg * nope), lambda i, g: (0, g)),
            pl.BlockSpec((hg * vd, lora), lambda i, g: (g, 0)),
        ],
        out_specs=[
            pl.BlockSpec((hg, tm, hd_pad), lambda i, g: (g, i, 0)),
            pl.BlockSpec((hg, vd + ones_rows, tm), lambda i, g: (g, 0, i)),
        ],
        out_shape=[
            jax.ShapeDtypeStruct((n_heads, s_len, hd_pad), BF16),
            jax.ShapeDtypeStruct((n_heads, vd + ones_rows, s_len), BF16),
        ],
        scratch_shapes=[
            pltpu.VMEM((tm, lora), BF16),
            pltpu.VMEM((lora, tm), BF16),
            pltpu.VMEM((tm, rope), BF16),
        ],
        compiler_params=_params("parallel", "arbitrary"),
        name="kv_proj",
    )(h, norm_g.reshape(1, d), wd_ext_bf16, lat_g.reshape(1, lora), cos2, sin2, wuk_bf16, wuvt_bf16)


def _q_kernel(h_ref, g_ref, wdt_ref, lat_ref, cos_ref, sin_ref, wut_ref, qt_ref, cqt_s,
              *, nope, rope, hg, q_scale):
    gidx = pl.program_id(1)

    @pl.when(gidx == 0)
    def _():
        xn = _rms(h_ref[...], g_ref[...]).astype(BF16)
        cqt = lax.dot_general(wdt_ref[...], xn, (((1,), (1,)), ((), ())),
                              preferred_element_type=F32)
        ms = jnp.mean(cqt * cqt, axis=0, keepdims=True)
        cqt_s[...] = (cqt * lax.rsqrt(ms + EPS) * lat_ref[...]).astype(BF16)

    qt = jnp.dot(wut_ref[...], cqt_s[...], preferred_element_type=F32) * q_scale
    cos_t = cos_ref[...]
    sin_t = sin_ref[...]
    hd = nope + rope
    half = rope // 2
    pad = qt_ref.shape[1] - hd
    for hh in range(hg):
        b = hh * hd
        qt_ref[hh, 0:nope] = qt[b:b + nope].astype(BF16)
        x1 = qt[b + nope:b + nope + half]
        x2 = qt[b + nope + half:b + hd]
        qt_ref[hh, nope:nope + half] = (x1 * cos_t - x2 * sin_t).astype(BF16)
        qt_ref[hh, nope + half:hd] = (x2 * cos_t + x1 * sin_t).astype(BF16)
        if pad:
            qt_ref[hh, hd:hd + pad] = jnp.zeros((pad, qt.shape[1]), BF16)


def _q_proj(h, norm_g, wdqt_bf16, lat_g, cos_t, sin_t, wuqt_bf16, *, n_heads, nope, rope, hd_pad,
            q_scale, tm=512, hg=8):
    s_len, d = h.shape
    qlora = wdqt_bf16.shape[0]
    hd = nope + rope
    hg = min(hg, n_heads)
    assert s_len % tm == 0 and n_heads % hg == 0 and wuqt_bf16.shape == (n_heads * hd, qlora)
    return pl.pallas_call(
        functools.partial(_q_kernel, nope=nope, rope=rope, hg=hg, q_scale=q_scale),
        grid=(s_len // tm, n_heads // hg),
        in_specs=[
            pl.BlockSpec((tm, d), lambda i, g: (i, 0)),
            _const_spec((1, d)),
            _const_spec((qlora, d)),
            _const_spec((qlora, 1)),
            pl.BlockSpec((rope // 2, tm), lambda i, g: (0, i)),
            pl.BlockSpec((rope // 2, tm), lambda i, g: (0, i)),
            pl.BlockSpec((hg * hd, qlora), lambda i, g: (g, 0)),
        ],
        out_specs=pl.BlockSpec((hg, hd_pad, tm), lambda i, g: (g, 0, i)),
        out_shape=jax.ShapeDtypeStruct((n_heads, hd_pad, s_len), BF16),
        scratch_shapes=[pltpu.VMEM((qlora, tm), BF16)],
        compiler_params=_params("parallel", "arbitrary"),
        name="q_proj",
    )(h, norm_g.reshape(1, d), wdqt_bf16, lat_g.reshape(qlora, 1), cos_t, sin_t, wuqt_bf16)


def _attn_kernel(qt_ref, k_ref, vt_ref, o_ref, s0_ref, s1_ref, acc_ref, m_ref, *, tq, tk, vd):
    i = pl.program_id(1)
    r = tq // tk
    s_buf = (s0_ref, s1_ref)

    def scores(j, slot):
        kblk = k_ref[0, pl.ds(pl.multiple_of(j * tk, tk), tk), :]
        s_buf[slot][...] = jnp.dot(kblk, qt_ref[0], preferred_element_type=F32)

    def stage(j, slot, diag_chunk=None, prefetch=True):
        if prefetch:
            scores(j + 1, 1 - slot)
        s = s_buf[slot][...]
        if diag_chunk is not None:
            kpos = lax.broadcasted_iota(jnp.int32, (tk, tq), 0) + diag_chunk * tk
            qpos = lax.broadcasted_iota(jnp.int32, (tk, tq), 1)
            s = jnp.where(kpos <= qpos, s, MASK_VALUE)
        m = m_ref[...]
        m_new = jnp.maximum(m, jnp.max(s, axis=0, keepdims=True))
        alpha = jnp.exp2(m - m_new)
        p = jnp.exp2(s - m_new)
        m_ref[...] = m_new
        vt = vt_ref[0, :, pl.ds(pl.multiple_of(j * tk, tk), tk)]
        acc_ref[...] = alpha * acc_ref[...] + jnp.dot(vt, p.astype(BF16), preferred_element_type=F32)

    m_ref[...] = jnp.full(m_ref.shape, MASK_VALUE, F32)
    acc_ref[...] = jnp.zeros(acc_ref.shape, F32)
    scores(0, 0)

    def group(g, carry):
        for c in range(r):
            stage(g * r + c, c % 2)
        return carry

    lax.fori_loop(0, i, group, 0)
    for c in range(r):
        stage(i * r + c, c % 2, diag_chunk=c, prefetch=c < r - 1)
    o_ref[...] = (acc_ref[0:vd] / acc_ref[vd:vd + 1]).T.astype(o_ref.dtype)


def _attention(qt, k, vt, *, vd, tq=1024, tk=256):
    n_heads, hd, s_len = qt.shape
    vda = vt.shape[1]
    tq = min(tq, s_len)
    assert k.shape == (n_heads, s_len, hd) and vt.shape == (n_heads, vda, s_len) and vda > vd
    assert s_len % tq == 0 and tq % (2 * tk) == 0
    return pl.pallas_call(
        functools.partial(_attn_kernel, tq=tq, tk=tk, vd=vd),
        grid=(n_heads, s_len // tq),
        in_specs=[
            pl.BlockSpec((1, hd, tq), lambda h, i: (h, 0, i)),
            pl.BlockSpec((1, s_len, hd), lambda h, i: (h, 0, 0)),
            pl.BlockSpec((1, vda, s_len), lambda h, i: (h, 0, 0)),
        ],
        out_specs=pl.BlockSpec((tq, vd), lambda h, i: (i, h)),
        out_shape=jax.ShapeDtypeStruct((s_len, n_heads * vd), BF16),
        scratch_shapes=[
            pltpu.VMEM((tk, tq), F32),
            pltpu.VMEM((tk, tq), F32),
            pltpu.VMEM((vda, tq), F32),
            pltpu.VMEM((1, tq), F32),
        ],
        compiler_params=_params("parallel", "parallel"),
        name="mla_attention",
    )(qt, k, vt)


def _oproj_kernel(a_ref, w_ref, r_ref, o_ref):
    o_ref[...] = r_ref[...] + jnp.dot(a_ref[...], w_ref[...], preferred_element_type=F32)


def _out_proj(a_bf16, w_bf16, resid, *, tm=512, tn=1024):
    s_len, kdim = a_bf16.shape
    d = w_bf16.shape[1]
    tn = min(tn, d)
    assert s_len % tm == 0 and d % tn == 0
    return pl.pallas_call(
        _oproj_kernel,
        grid=(s_len // tm, d // tn),
        in_specs=[
            pl.BlockSpec((tm, kdim), lambda i, j: (i, 0)),
            pl.BlockSpec((kdim, tn), lambda i, j: (0, j)),
            pl.BlockSpec((tm, tn), lambda i, j: (i, j)),
        ],
        out_specs=pl.BlockSpec((tm, tn), lambda i, j: (i, j)),
        out_shape=jax.ShapeDtypeStruct((s_len, d), F32),
        compiler_params=_params("parallel", "parallel"),
        name="attn_out_proj",
    )(a_bf16, w_bf16, resid)


def _trunk(x, a_norm, a_pool_w, a_scale, kv_norm, w_dkv, kv_lat_norm, w_ukv, b_norm, w_dq, q_lat_norm,
           w_uq, w_o, ffn_norm, ffn_w_in, ffn_conv_w, ffn_conv_b, ffn_w_out, final_norm,
           *, pool_ts=256, ffn_tm=512, ffn_tf=256, proj_tm=512, attn_tq=1024, attn_tk=256,
           oproj_tm=512):
    batch, s_len, d = x.shape
    depth = ffn_norm.shape[0]
    n_a = a_norm.shape[0]
    n_b = b_norm.shape[0]
    assert n_a + n_b == depth and n_a >= 1
    lora, n_heads, kvd = w_ukv.shape
    hd = w_uq.shape[-1]
    rope = w_dkv.shape[1] - lora
    nope = hd - rope
    vd = kvd - nope
    half = rope // 2
    hd_pad = -(-hd // LANES) * LANES

    inv_freq = ROPE_THETA ** (-jnp.arange(0, rope, 2, dtype=F32) / rope)
    ang = jnp.arange(s_len, dtype=F32)[:, None] * inv_freq[None, :]
    cos, sin = jnp.cos(ang), jnp.sin(ang)
    cos2 = jnp.concatenate([cos, cos], axis=1)
    sin2 = jnp.concatenate([-sin, sin], axis=1)
    cos_t, sin_t = cos.T, sin.T

    bf = lambda w: w.astype(BF16)
    wd_ext = bf(jnp.concatenate([w_dkv, w_dkv[:, lora + half:], w_dkv[:, lora:lora + half]], axis=1))
    wuk = bf(w_ukv[:, :, :nope].reshape(lora, n_heads * nope))
    wuvt = bf(w_ukv[:, :, nope:].reshape(lora, n_heads * vd).T)
    final_g = final_norm

    outs = []
    for b in range(batch):
        h = x[b]
        shared = None
        for layer in range(depth):
            last = layer == depth - 1
            if layer < n_a:
                h = _pool_layer(h, a_norm[layer], bf(a_pool_w[layer]), a_scale[layer], ts=pool_ts)
            else:
                j = layer - n_a
                qlora = w_dq.shape[2]
                qt = _q_proj(h, b_norm[j], bf(w_dq[j].T), q_lat_norm[j], cos_t, sin_t,
                             bf(w_uq[j].reshape(qlora, n_heads * hd).T),
                             n_heads=n_heads, nope=nope, rope=rope, hd_pad=hd_pad,
                             q_scale=hd ** -0.5 * math.log2(math.e), tm=proj_tm)
                o = _attention(qt, shared[0], shared[1], vd=vd, tq=attn_tq, tk=attn_tk)
                h = _out_proj(o, bf(w_o[j]), h, tm=oproj_tm)
            h = _ffn_layer(h, ffn_norm[layer], bf(ffn_w_in[layer]), ffn_conv_w[layer], ffn_conv_b[layer],
                           bf(ffn_w_out[layer]), final_g, final=last, tm=ffn_tm, tf=ffn_tf)
            if layer == n_a - 1:
                shared = _kv_proj(h, kv_norm, wd_ext, kv_lat_norm, cos2, sin2, wuk, wuvt,
                                  n_heads=n_heads, nope=nope, rope=rope, hd_pad=hd_pad,
                                  ones_rows=BF16_SUBLANES, tm=proj_tm)
        outs.append(h)
    return jnp.stack(outs, axis=0)


def kernel(x, a_norm, a_pool_w, a_scale, kv_norm, w_dkv, kv_lat_norm, w_ukv, b_norm, w_dq, q_lat_norm,
           w_uq, w_o, ffn_norm, ffn_w_in, ffn_conv_w, ffn_conv_b, ffn_w_out, final_norm):
    return _trunk(x, a_norm, a_pool_w, a_scale, kv_norm, w_dkv, kv_lat_norm, w_ukv, b_norm, w_dq,
                  q_lat_norm, w_uq, w_o, ffn_norm, ffn_w_in, ffn_conv_w, ffn_conv_b, ffn_w_out,
                  final_norm)
```

```python
import functools
import math

import jax
import jax.numpy as jnp
from jax import lax
from jax.experimental import pallas as pl
from jax.experimental.pallas import tpu as pltpu

EPS = 1e-6
ROPE_THETA = 10000.0
POOL_WINDOWS = (2, 4, 8, 16)
CONV_WIDTH = 3
POOL_HALO = 16
CONV_HALO = 8
CONV_PIECES = 8
MASK_VALUE = -1e30
VMEM_LIMIT_BYTES = 56 * 1024 * 1024
LANES = 128
BF16_SUBLANES = 16

F32 = jnp.float32
BF16 = jnp.bfloat16


def _rms(x, g):
    return x * lax.rsqrt(jnp.mean(x * x, axis=-1, keepdims=True) + EPS) * g


def _params(*sem):
    return pltpu.CompilerParams(dimension_semantics=sem, vmem_limit_bytes=VMEM_LIMIT_BYTES)


def _const_spec(shape):
    nd = len(shape)
    return pl.BlockSpec(shape, lambda *_: (0,) * nd, pipeline_mode=pl.Buffered(1))


def _pool_kernel(x_ref, halo_ref, g_ref, w_ref, sc_ref, o_ref, *, ts, gc):
    i = pl.program_id(0)
    g = g_ref[...]
    x = x_ref[...]
    xn = _rms(x, g)
    hn = jnp.where(i == 0, 0.0, _rms(halo_ref[...], g))
    pos = i * ts + lax.broadcasted_iota(jnp.int32, (ts, 1), 0)
    for gi, w in enumerate(POOL_WINDOWS):
        sl = slice(gi * gc, (gi + 1) * gc)
        xg = xn[:, sl]
        s = jnp.concatenate([hn[:, sl], xg], axis=0)
        k = 1
        while k < w:
            s = s + pltpu.roll(s, k, axis=0)
            k *= 2
        cnt = jnp.minimum(pos + 1, w).astype(F32)
        pooled = s[POOL_HALO:] / cnt - xg
        mixed = jnp.dot(pooled.astype(BF16), w_ref[gi], preferred_element_type=F32)
        o_ref[:, sl] = x[:, sl] + mixed * sc_ref[:, sl]


def _pool_layer(x, norm_g, w_bf16, scale, *, ts=256):
    s_len, d = x.shape
    ng, gc, _ = w_bf16.shape
    assert ng == len(POOL_WINDOWS) and ng * gc == d and s_len % ts == 0 and ts % POOL_HALO == 0
    per = ts // POOL_HALO
    return pl.pallas_call(
        functools.partial(_pool_kernel, ts=ts, gc=gc),
        grid=(s_len // ts,),
        in_specs=[
            pl.BlockSpec((ts, d), lambda i: (i, 0)),
            pl.BlockSpec((POOL_HALO, d), lambda i: (jnp.maximum(i * per - 1, 0), 0)),
            _const_spec((1, d)),
            _const_spec((ng, gc, gc)),
            _const_spec((1, d)),
        ],
        out_specs=pl.BlockSpec((ts, d), lambda i: (i, 0)),
        out_shape=jax.ShapeDtypeStruct((s_len, d), F32),
        compiler_params=_params("parallel"),
        name="pool_layer",
    )(x, x, norm_g.reshape(1, d), w_bf16, scale.reshape(1, d))


def _ffn_kernel(h_ref, g_ref, wg_ref, wv_ref, cwg_ref, cwv_ref, bg_ref, bv_ref, wo_ref, fg_ref,
                o_ref, xn_s, a0_s, a1_s, ug0_s, ug1_s, uv0_s, uv1_s, cg_s, cv_s, *, tm, tn, nf, final):
    i = pl.program_id(0)
    f = pl.program_id(1)
    acts = (a0_s, a1_s)
    ugs = (ug0_s, ug1_s)
    uvs = (uv0_s, uv1_s)

    @pl.when(f == 0)
    def _():
        h = h_ref[...]
        xn_s[...] = _rms(h, g_ref[...]).astype(BF16)
        o_ref[...] = h

    c = jnp.clip(f - 1, 0, nf - 1)

    @pl.when((i == 0) & (f >= 1) & (f <= nf))
    def _():
        cg_s[c] = jnp.zeros(cg_s.shape[1:], F32)
        cv_s[c] = jnp.zeros(cv_s.shape[1:], F32)

    n_down = o_ref.shape[1] // tn
    rows = tm // CONV_PIECES

    def up_piece(p, which):
        u_s, w_ref = ((ugs[p], wg_ref), (uvs[p], wv_ref))[which]
        u_s[...] = jnp.dot(xn_s[...], w_ref[...], preferred_element_type=F32)

    def conv_rows(u_s, cw_ref, b_ref, c_s, r0):
        prev = c_s[c] if r0 == 0 else u_s[r0 - CONV_HALO:r0]
        u = u_s[r0:r0 + rows]
        cat = jnp.concatenate([prev, u], axis=0)
        u1 = pltpu.roll(cat, 1, axis=0)[CONV_HALO:]
        u2 = pltpu.roll(cat, 2, axis=0)[CONV_HALO:]
        cw = cw_ref[...]
        return cw[0:1] * u2 + cw[1:2] * u1 + cw[2:3] * u + b_ref[...]

    def conv_piece(p, k):
        r0 = k * rows
        gate = conv_rows(ugs[p], cwg_ref, bg_ref, cg_s, r0)
        val = conv_rows(uvs[p], cwv_ref, bv_ref, cv_s, r0)
        acts[p][r0:r0 + rows] = (gate * (1.0 / (1.0 + jnp.exp(-gate))) * val).astype(BF16)
        if k == CONV_PIECES - 1:
            cg_s[c] = ugs[p][tm - CONV_HALO:]
            cv_s[c] = uvs[p][tm - CONV_HALO:]

    def down_piece(p, n):
        sl = slice(n * tn, (n + 1) * tn)
        o_ref[:, sl] += jnp.dot(acts[p][...], wo_ref[:, sl], preferred_element_type=F32)

    def run(p_up=None, p_conv=None, p_down=None):
        mxu, vpu = [], []
        if p_up is not None:
            mxu += [functools.partial(up_piece, p_up, w) for w in range(2)]
        if p_down is not None:
            mxu += [functools.partial(down_piece, p_down, n) for n in range(n_down)]
        if p_conv is not None:
            vpu += [functools.partial(conv_piece, p_conv, k) for k in range(CONV_PIECES)]
        for idx in range(max(len(mxu), len(vpu))):
            if idx < len(mxu):
                mxu[idx]()
            if idx < len(vpu):
                vpu[idx]()

    @pl.when(f == 0)
    def _():
        run(p_up=0)

    @pl.when(f == 1)
    def _():
        run(p_up=1, p_conv=0)

    for p in range(2):
        @pl.when((f >= 2) & (f < nf) & (f % 2 == p))
        def _():
            run(p_up=p, p_conv=1 - p, p_down=p)

    @pl.when(f == nf)
    def _():
        run(p_conv=(nf - 1) % 2, p_down=nf % 2)

    @pl.when(f == nf + 1)
    def _():
        run(p_down=(nf - 1) % 2)
        if final:
            o_ref[...] = _rms(o_ref[...], fg_ref[...])


def _ffn_layer(h, norm_g, w_in_bf16, conv_w, conv_b, w_out_bf16, final_g, *, final, tm=512, tf=256,
               tn=512):
    s_len, d = h.shape
    dff = w_out_bf16.shape[0]
    tn = min(tn, d)
    assert s_len % tm == 0 and dff % tf == 0 and d % tn == 0 and w_in_bf16.shape == (d, 2 * dff)
    nf = dff // tf
    assert nf >= 2
    conv_b2 = conv_b.reshape(1, 2 * dff)
    up = lambda f: jnp.minimum(f, nf - 1)
    cv = lambda f: jnp.clip(f - 1, 0, nf - 1)
    dn = lambda f: jnp.clip(f - 2, 0, nf - 1)
    return pl.pallas_call(
        functools.partial(_ffn_kernel, tm=tm, tn=tn, nf=nf, final=final),
        grid=(s_len // tm, nf + 2),
        in_specs=[
            pl.BlockSpec((tm, d), lambda i, f: (i, 0), pipeline_mode=pl.Buffered(1)),
            _const_spec((1, d)),
            pl.BlockSpec((d, tf), lambda i, f: (0, up(f))),
            pl.BlockSpec((d, tf), lambda i, f: (0, nf + up(f))),
            pl.BlockSpec((CONV_WIDTH, tf), lambda i, f: (0, cv(f))),
            pl.BlockSpec((CONV_WIDTH, tf), lambda i, f: (0, nf + cv(f))),
            pl.BlockSpec((1, tf), lambda i, f: (0, cv(f))),
            pl.BlockSpec((1, tf), lambda i, f: (0, nf + cv(f))),
            pl.BlockSpec((tf, d), lambda i, f: (dn(f), 0)),
            _const_spec((1, d)),
        ],
        out_specs=pl.BlockSpec((tm, d), lambda i, f: (i, 0)),
        out_shape=jax.ShapeDtypeStruct((s_len, d), F32),
        scratch_shapes=[
            pltpu.VMEM((tm, d), BF16),
            pltpu.VMEM((tm, tf), BF16),
            pltpu.VMEM((tm, tf), BF16),
            pltpu.VMEM((tm, tf), F32),
            pltpu.VMEM((tm, tf), F32),
            pltpu.VMEM((tm, tf), F32),
            pltpu.VMEM((tm, tf), F32),
            pltpu.VMEM((nf, CONV_HALO, tf), F32),
            pltpu.VMEM((nf, CONV_HALO, tf), F32),
        ],
        compiler_params=_params("arbitrary", "arbitrary"),
        name="conv_ffn",
    )(h, norm_g.reshape(1, d), w_in_bf16, w_in_bf16, conv_w, conv_w, conv_b2, conv_b2, w_out_bf16,
      final_g.reshape(1, d))


def _kv_kernel(h_ref, g_ref, wd_ref, lat_ref, cos_ref, sin_ref, wuk_ref, wuvt_ref,
               k_ref, vt_ref, c_s, ct_s, kr_s, *, lora, rope, nope, hg):
    gidx = pl.program_id(1)

    @pl.when(gidx == 0)
    def _():
        hn = _rms(h_ref[...], g_ref[...]).astype(BF16)
        ckv = jnp.dot(hn, wd_ref[...], preferred_element_type=F32)
        c = _rms(ckv[:, :lora], lat_ref[...])
        c_s[...] = c.astype(BF16)
        ct_s[...] = c.T.astype(BF16)
        kr = ckv[:, lora:lora + rope] * cos_ref[...] + ckv[:, lora + rope:] * sin_ref[...]
        kr_s[...] = kr.astype(BF16)

    kn = jnp.dot(c_s[...], wuk_ref[...], preferred_element_type=F32)
    vt = jnp.dot(wuvt_ref[...], ct_s[...], preferred_element_type=F32)
    vd = vt.shape[0] // hg
    tm = kn.shape[0]
    hd = nope + rope
    k_pad = k_ref.shape[2] - hd
    ones_rows = vt_ref.shape[1] - vd
    for hh in range(hg):
        k_ref[hh, :, 0:nope] = kn[:, hh * nope:(hh + 1) * nope].astype(BF16)
        k_ref[hh, :, nope:hd] = kr_s[...]
        if k_pad:
            k_ref[hh, :, hd:hd + k_pad] = jnp.zeros((tm, k_pad), BF16)
        vt_ref[hh, 0:vd] = vt[hh * vd:(hh + 1) * vd].astype(BF16)
        vt_ref[hh, vd:vd + ones_rows] = jnp.ones((ones_rows, tm), BF16)


def _kv_proj(h, norm_g, wd_ext_bf16, lat_g, cos2, sin2, wuk_bf16, wuvt_bf16, *, n_heads, nope, rope,
             hd_pad, ones_rows, tm=512, hg=8):
    s_len, d = h.shape
    lora = wuk_bf16.shape[0]
    vd = wuvt_bf16.shape[0] // n_heads
    hg = min(hg, n_heads)
    assert s_len % tm == 0 and n_heads % hg == 0 and wd_ext_bf16.shape == (d, lora + 2 * rope)
    return pl.pallas_call(
        functools.partial(_kv_kernel, lora=lora, rope=rope, nope=nope, hg=hg),
        grid=(s_len // tm, n_heads // hg),
        in_specs=[
            pl.BlockSpec((tm, d), lambda i, g: (i, 0)),
            _const_spec((1, d)),
            _const_spec((d, lora + 2 * rope)),
            _const_spec((1, lora)),
            pl.BlockSpec((tm, rope), lambda i, g: (i, 0)),
            pl.BlockSpec((tm, rope), lambda i, g: (i, 0)),
            pl.BlockSpec((lora, hg * nope), lambda i, g: (0, g)),
            pl.BlockSpec((hg * vd, lora), lambda i, g: (g, 0)),
        ],
        out_specs=[
            pl.BlockSpec((hg, tm, hd_pad), lambda i, g: (g, i, 0)),
            pl.BlockSpec((hg, vd + ones_rows, tm), lambda i, g: (g, 0, i)),
        ],
        out_shape=[
            jax.ShapeDtypeStruct((n_heads, s_len, hd_pad), BF16),
            jax.ShapeDtypeStruct((n_heads, vd + ones_rows, s_len), BF16),
        ],
        scratch_shapes=[
            pltpu.VMEM((tm, lora), BF16),
            pltpu.VMEM((lora, tm), BF16),
            pltpu.VMEM((tm, rope), BF16),
        ],
        compiler_params=_params("parallel", "arbitrary"),
        name="kv_proj",
    )(h, norm_g.reshape(1, d), wd_ext_bf16, lat_g.reshape(1, lora), cos2, sin2, wuk_bf16, wuvt_bf16)


def _q_kernel(h_ref, g_ref, wdt_ref, lat_ref, cos_ref, sin_ref, wut_ref, qt_ref, cqt_s,
              *, nope, rope, hg, q_scale):
    gidx = pl.program_id(1)

    @pl.when(gidx == 0)
    def _():
        xn = _rms(h_ref[...], g_ref[...]).astype(BF16)
        cqt = lax.dot_general(wdt_ref[...], xn, (((1,), (1,)), ((), ())),
                              preferred_element_type=F32)
        ms = jnp.mean(cqt * cqt, axis=0, keepdims=True)
        cqt_s[...] = (cqt * lax.rsqrt(ms + EPS) * lat_ref[...]).astype(BF16)

    qt = jnp.dot(wut_ref[...], cqt_s[...], preferred_element_type=F32) * q_scale
    cos_t = cos_ref[...]
    sin_t = sin_ref[...]
    hd = nope + rope
    half = rope // 2
    pad = qt_ref.shape[1] - hd
    for hh in range(hg):
        b = hh * hd
        qt_ref[hh, 0:nope] = qt[b:b + nope].astype(BF16)
        x1 = qt[b + nope:b + nope + half]
        x2 = qt[b + nope + half:b + hd]
        qt_ref[hh, nope:nope + half] = (x1 * cos_t - x2 * sin_t).astype(BF16)
        qt_ref[hh, nope + half:hd] = (x2 * cos_t + x1 * sin_t).astype(BF16)
        if pad:
            qt_ref[hh, hd:hd + pad] = jnp.zeros((pad, qt.shape[1]), BF16)


def _q_proj(h, norm_g, wdqt_bf16, lat_g, cos_t, sin_t, wuqt_bf16, *, n_heads, nope, rope, hd_pad,
            q_scale, tm=512, hg=8):
    s_len, d = h.shape
    qlora = wdqt_bf16.shape[0]
    hd = nope + rope
    hg = min(hg, n_heads)
    assert s_len % tm == 0 and n_heads % hg == 0 and wuqt_bf16.shape == (n_heads * hd, qlora)
    return pl.pallas_call(
        functools.partial(_q_kernel, nope=nope, rope=rope, hg=hg, q_scale=q_scale),
        grid=(s_len // tm, n_heads // hg),
        in_specs=[
            pl.BlockSpec((tm, d), lambda i, g: (i, 0)),
            _const_spec((1, d)),
            _const_spec((qlora, d)),
            _const_spec((qlora, 1)),
            pl.BlockSpec((rope // 2, tm), lambda i, g: (0, i)),
            pl.BlockSpec((rope // 2, tm), lambda i, g: (0, i)),
            pl.BlockSpec((hg * hd, qlora), lambda i, g: (g, 0)),
        ],
        out_specs=pl.BlockSpec((hg, hd_pad, tm), lambda i, g: (g, 0, i)),
        out_shape=jax.ShapeDtypeStruct((n_heads, hd_pad, s_len), BF16),
        scratch_shapes=[pltpu.VMEM((qlora, tm), BF16)],
        compiler_params=_params("parallel", "arbitrary"),
        name="q_proj",
    )(h, norm_g.reshape(1, d), wdqt_bf16, lat_g.reshape(qlora, 1), cos_t, sin_t, wuqt_bf16)


def _attn_kernel(qt_ref, k_ref, vt_ref, o_ref, s0_ref, s1_ref, acc_ref, m_ref, *, tq, tk, vd):
    i = pl.program_id(1)
    r = tq // tk
    s_buf = (s0_ref, s1_ref)

    def scores(j, slot, lo):
        kblk = k_ref[0, pl.ds(pl.multiple_of(j * tk, tk), tk), :]
        s_buf[slot][:, lo:] = jnp.dot(kblk, qt_ref[0, :, lo:], preferred_element_type=F32)

    def stage(j, slot, lo=0, diagonal=False, next_lo=0):
        if next_lo is not None:
            scores(j + 1, 1 - slot, next_lo)
        s = s_buf[slot][:, lo:]
        if diagonal:
            kpos = lax.broadcasted_iota(jnp.int32, s.shape, 0)
            qpos = lax.broadcasted_iota(jnp.int32, s.shape, 1)
            s = jnp.where(kpos <= qpos, s, MASK_VALUE)
        m = m_ref[:, lo:]
        m_new = jnp.maximum(m, jnp.max(s, axis=0, keepdims=True))
        alpha = jnp.exp2(m - m_new)
        p = jnp.exp2(s - m_new)
        m_ref[:, lo:] = m_new
        vt = vt_ref[0, :, pl.ds(pl.multiple_of(j * tk, tk), tk)]
        acc_ref[:, lo:] = alpha * acc_ref[:, lo:] + jnp.dot(vt, p.astype(BF16),
                                                           preferred_element_type=F32)

    m_ref[...] = jnp.full(m_ref.shape, MASK_VALUE, F32)
    acc_ref[...] = jnp.zeros(acc_ref.shape, F32)
    scores(0, 0, 0)

    def group(g, carry):
        for c in range(r):
            stage(g * r + c, c % 2)
        return carry

    lax.fori_loop(0, i, group, 0)
    for c in range(r):
        stage(i * r + c, c % 2, lo=c * tk, diagonal=True, next_lo=(c + 1) * tk if c < r - 1 else None)
    o_ref[...] = (acc_ref[0:vd] / acc_ref[vd:vd + 1]).T.astype(o_ref.dtype)


def _attention(qt, k, vt, *, vd, tq=1024, tk=256):
    n_heads, hd, s_len = qt.shape
    vda = vt.shape[1]
    tq = min(tq, s_len)
    assert k.shape == (n_heads, s_len, hd) and vt.shape == (n_heads, vda, s_len) and vda > vd
    assert s_len % tq == 0 and tq % (2 * tk) == 0
    return pl.pallas_call(
        functools.partial(_attn_kernel, tq=tq, tk=tk, vd=vd),
        grid=(n_heads, s_len // tq),
        in_specs=[
            pl.BlockSpec((1, hd, tq), lambda h, i: (h, 0, i)),
            pl.BlockSpec((1, s_len, hd), lambda h, i: (h, 0, 0)),
            pl.BlockSpec((1, vda, s_len), lambda h, i: (h, 0, 0)),
        ],
        out_specs=pl.BlockSpec((tq, vd), lambda h, i: (i, h)),
        out_shape=jax.ShapeDtypeStruct((s_len, n_heads * vd), BF16),
        scratch_shapes=[
            pltpu.VMEM((tk, tq), F32),
            pltpu.VMEM((tk, tq), F32),
            pltpu.VMEM((vda, tq), F32),
            pltpu.VMEM((1, tq), F32),
        ],
        compiler_params=_params("parallel", "parallel"),
        name="mla_attention",
    )(qt, k, vt)


def _oproj_kernel(a_ref, w_ref, r_ref, o_ref):
    o_ref[...] = r_ref[...] + jnp.dot(a_ref[...], w_ref[...], preferred_element_type=F32)


def _out_proj(a_bf16, w_bf16, resid, *, tm=512, tn=1024):
    s_len, kdim = a_bf16.shape
    d = w_bf16.shape[1]
    tn = min(tn, d)
    assert s_len % tm == 0 and d % tn == 0
    return pl.pallas_call(
        _oproj_kernel,
        grid=(s_len // tm, d // tn),
        in_specs=[
            pl.BlockSpec((tm, kdim), lambda i, j: (i, 0)),
            pl.BlockSpec((kdim, tn), lambda i, j: (0, j)),
            pl.BlockSpec((tm, tn), lambda i, j: (i, j)),
        ],
        out_specs=pl.BlockSpec((tm, tn), lambda i, j: (i, j)),
        out_shape=jax.ShapeDtypeStruct((s_len, d), F32),
        compiler_params=_params("parallel", "parallel"),
        name="attn_out_proj",
    )(a_bf16, w_bf16, resid)


def _trunk(x, a_norm, a_pool_w, a_scale, kv_norm, w_dkv, kv_lat_norm, w_ukv, b_norm, w_dq, q_lat_norm,
           w_uq, w_o, ffn_norm, ffn_w_in, ffn_conv_w, ffn_conv_b, ffn_w_out, final_norm,
           *, pool_ts=256, ffn_tm=512, ffn_tf=256, proj_tm=512, attn_tq=2048, attn_tk=512,
           oproj_tm=512):
    batch, s_len, d = x.shape
    depth = ffn_norm.shape[0]
    n_a = a_norm.shape[0]
    n_b = b_norm.shape[0]
    assert n_a + n_b == depth and n_a >= 1
    lora, n_heads, kvd = w_ukv.shape
    hd = w_uq.shape[-1]
    rope = w_dkv.shape[1] - lora
    nope = hd - rope
    vd = kvd - nope
    half = rope // 2
    hd_pad = -(-hd // LANES) * LANES

    inv_freq = ROPE_THETA ** (-jnp.arange(0, rope, 2, dtype=F32) / rope)
    ang = jnp.arange(s_len, dtype=F32)[:, None] * inv_freq[None, :]
    cos, sin = jnp.cos(ang), jnp.sin(ang)
    cos2 = jnp.concatenate([cos, cos], axis=1)
    sin2 = jnp.concatenate([-sin, sin], axis=1)
    cos_t, sin_t = cos.T, sin.T

    bf = lambda w: w.astype(BF16)
    wd_ext = bf(jnp.concatenate([w_dkv, w_dkv[:, lora + half:], w_dkv[:, lora:lora + half]], axis=1))
    wuk = bf(w_ukv[:, :, :nope].reshape(lora, n_heads * nope))
    wuvt = bf(w_ukv[:, :, nope:].reshape(lora, n_heads * vd).T)
    final_g = final_norm

    outs = []
    for b in range(batch):
        h = x.reshape(s_len, d) if batch == 1 else x[b]
        shared = None
        for layer in range(depth):
            last = layer == depth - 1
            if layer < n_a:
                h = _pool_layer(h, a_norm[layer], bf(a_pool_w[layer]), a_scale[layer], ts=pool_ts)
            else:
                j = layer - n_a
                qlora = w_dq.shape[2]
                qt = _q_proj(h, b_norm[j], bf(w_dq[j].T), q_lat_norm[j], cos_t, sin_t,
                             bf(w_uq[j].reshape(qlora, n_heads * hd).T),
                             n_heads=n_heads, nope=nope, rope=rope, hd_pad=hd_pad,
                             q_scale=hd ** -0.5 * math.log2(math.e), tm=proj_tm)
                o = _attention(qt, shared[0], shared[1], vd=vd, tq=attn_tq, tk=attn_tk)
                h = _out_proj(o, bf(w_o[j]), h, tm=oproj_tm)
            h = _ffn_layer(h, ffn_norm[layer], bf(ffn_w_in[layer]), ffn_conv_w[layer], ffn_conv_b[layer],
                           bf(ffn_w_out[layer]), final_g, final=last, tm=ffn_tm, tf=ffn_tf)
            if layer == n_a - 1:
                shared = _kv_proj(h, kv_norm, wd_ext, kv_lat_norm, cos2, sin2, wuk, wuvt,
                                  n_heads=n_heads, nope=nope, rope=rope, hd_pad=hd_pad,
                                  ones_rows=BF16_SUBLANES, tm=proj_tm)
        outs.append(h)
    return outs[0].reshape(1, s_len, d) if batch == 1 else jnp.stack(outs, axis=0)


def kernel(x, a_norm, a_pool_w, a_scale, kv_norm, w_dkv, kv_lat_norm, w_ukv, b_norm, w_dq, q_lat_norm,
           w_uq, w_o, ffn_norm, ffn_w_in, ffn_conv_w, ffn_conv_b, ffn_w_out, final_norm):
    return _trunk(x, a_norm, a_pool_w, a_scale, kv_norm, w_dkv, kv_lat_norm, w_ukv, b_norm, w_dq,
                  q_lat_norm, w_uq, w_o, ffn_norm, ffn_w_in, ffn_conv_w, ffn_conv_b, ffn_w_out,
                  final_norm)
```

```python
import functools
import math

import jax
import jax.numpy as jnp
from jax import lax
from jax.experimental import pallas as pl
from jax.experimental.pallas import tpu as pltpu

EPS = 1e-6
ROPE_THETA = 10000.0
POOL_WINDOWS = (2, 4, 8, 16)
CONV_WIDTH = 3
POOL_HALO = 16
CONV_HALO = 8
CONV_PIECES = 8
MASK_VALUE = -1e30
VMEM_LIMIT_BYTES = 56 * 1024 * 1024
LANES = 128
BF16_SUBLANES = 16

F32 = jnp.float32
BF16 = jnp.bfloat16


def _rms(x, g):
    return x * lax.rsqrt(jnp.mean(x * x, axis=-1, keepdims=True) + EPS) * g


def _params(*sem):
    return pltpu.CompilerParams(dimension_semantics=sem, vmem_limit_bytes=VMEM_LIMIT_BYTES)


def _const_spec(shape):
    nd = len(shape)
    return pl.BlockSpec(shape, lambda *_: (0,) * nd, pipeline_mode=pl.Buffered(1))


def _pool_kernel(x_ref, halo_ref, g_ref, w_ref, sc_ref, o_ref, *, ts, gc):
    i = pl.program_id(0)
    g = g_ref[...]
    x = x_ref[...]
    xn = _rms(x, g)
    hn = jnp.where(i == 0, 0.0, _rms(halo_ref[...], g))
    pos = i * ts + lax.broadcasted_iota(jnp.int32, (ts, 1), 0)
    for gi, w in enumerate(POOL_WINDOWS):
        sl = slice(gi * gc, (gi + 1) * gc)
        xg = xn[:, sl]
        s = jnp.concatenate([hn[:, sl], xg], axis=0)
        k = 1
        while k < w:
            s = s + pltpu.roll(s, k, axis=0)
            k *= 2
        cnt = jnp.minimum(pos + 1, w).astype(F32)
        pooled = s[POOL_HALO:] / cnt - xg
        mixed = jnp.dot(pooled.astype(BF16), w_ref[gi], preferred_element_type=F32)
        o_ref[:, sl] = x[:, sl] + mixed * sc_ref[:, sl]


def _pool_layer(x, norm_g, w_bf16, scale, *, ts=256):
    s_len, d = x.shape
    ng, gc, _ = w_bf16.shape
    assert ng == len(POOL_WINDOWS) and ng * gc == d and s_len % ts == 0 and ts % POOL_HALO == 0
    per = ts // POOL_HALO
    return pl.pallas_call(
        functools.partial(_pool_kernel, ts=ts, gc=gc),
        grid=(s_len // ts,),
        in_specs=[
            pl.BlockSpec((ts, d), lambda i: (i, 0)),
            pl.BlockSpec((POOL_HALO, d), lambda i: (jnp.maximum(i * per - 1, 0), 0)),
            _const_spec((1, d)),
            _const_spec((ng, gc, gc)),
            _const_spec((1, d)),
        ],
        out_specs=pl.BlockSpec((ts, d), lambda i: (i, 0)),
        out_shape=jax.ShapeDtypeStruct((s_len, d), F32),
        compiler_params=_params("parallel"),
        name="pool_layer",
    )(x, x, norm_g.reshape(1, d), w_bf16, scale.reshape(1, d))


def _ffn_kernel(h_ref, g_ref, wg_ref, wv_ref, cwg_ref, cwv_ref, bg_ref, bv_ref, wo_ref, fg_ref,
                o_ref, xn_s, a0_s, a1_s, ug0_s, ug1_s, uv0_s, uv1_s, cg_s, cv_s, *, tm, tn, nf, final):
    i = pl.program_id(0)
    f = pl.program_id(1)
    acts = (a0_s, a1_s)
    ugs = (ug0_s, ug1_s)
    uvs = (uv0_s, uv1_s)

    @pl.when(f == 0)
    def _():
        h = h_ref[...]
        xn_s[...] = _rms(h, g_ref[...]).astype(BF16)
        o_ref[...] = h

    c = jnp.clip(f - 1, 0, nf - 1)

    @pl.when((i == 0) & (f >= 1) & (f <= nf))
    def _():
        cg_s[c] = jnp.zeros(cg_s.shape[1:], F32)
        cv_s[c] = jnp.zeros(cv_s.shape[1:], F32)

    n_down = o_ref.shape[1] // tn
    rows = tm // CONV_PIECES

    def up_piece(p, which):
        u_s, w_ref = ((ugs[p], wg_ref), (uvs[p], wv_ref))[which]
        u_s[...] = jnp.dot(xn_s[...], w_ref[...], preferred_element_type=F32)

    def conv_rows(u_s, cw_ref, b_ref, c_s, r0):
        prev = c_s[c] if r0 == 0 else u_s[r0 - CONV_HALO:r0]
        u = u_s[r0:r0 + rows]
        cat = jnp.concatenate([prev, u], axis=0)
        u1 = pltpu.roll(cat, 1, axis=0)[CONV_HALO:]
        u2 = pltpu.roll(cat, 2, axis=0)[CONV_HALO:]
        cw = cw_ref[...]
        return cw[0:1] * u2 + cw[1:2] * u1 + cw[2:3] * u + b_ref[...]

    def conv_piece(p, k):
        r0 = k * rows
        gate = conv_rows(ugs[p], cwg_ref, bg_ref, cg_s, r0)
        val = conv_rows(uvs[p], cwv_ref, bv_ref, cv_s, r0)
        acts[p][r0:r0 + rows] = (gate * (1.0 / (1.0 + jnp.exp(-gate))) * val).astype(BF16)
        if k == CONV_PIECES - 1:
            cg_s[c] = ugs[p][tm - CONV_HALO:]
            cv_s[c] = uvs[p][tm - CONV_HALO:]

    def down_piece(p, n):
        sl = slice(n * tn, (n + 1) * tn)
        o_ref[:, sl] += jnp.dot(acts[p][...], wo_ref[:, sl], preferred_element_type=F32)

    def run(p_up=None, p_conv=None, p_down=None):
        mxu, vpu = [], []
        if p_up is not None:
            mxu += [functools.partial(up_piece, p_up, w) for w in range(2)]
        if p_down is not None:
            mxu += [functools.partial(down_piece, p_down, n) for n in range(n_down)]
        if p_conv is not None:
            vpu += [functools.partial(conv_piece, p_conv, k) for k in range(CONV_PIECES)]
        for idx in range(max(len(mxu), len(vpu))):
            if idx < len(mxu):
                mxu[idx]()
            if idx < len(vpu):
                vpu[idx]()

    @pl.when(f == 0)
    def _():
        run(p_up=0)

    @pl.when(f == 1)
    def _():
        run(p_up=1, p_conv=0)

    for p in range(2):
        @pl.when((f >= 2) & (f < nf) & (f % 2 == p))
        def _():
            run(p_up=p, p_conv=1 - p, p_down=p)

    @pl.when(f == nf)
    def _():
        run(p_conv=(nf - 1) % 2, p_down=nf % 2)

    @pl.when(f == nf + 1)
    def _():
        run(p_down=(nf - 1) % 2)
        if final:
            o_ref[...] = _rms(o_ref[...], fg_ref[...])


def _ffn_layer(h, norm_g, w_in_bf16, conv_w, conv_b, w_out_bf16, final_g, *, final, tm=512, tf=256,
               tn=512):
    s_len, d = h.shape
    dff = w_out_bf16.shape[0]
    tn = min(tn, d)
    assert s_len % tm == 0 and dff % tf == 0 and d % tn == 0 and w_in_bf16.shape == (d, 2 * dff)
    nf = dff // tf
    assert nf >= 2
    conv_b2 = conv_b.reshape(1, 2 * dff)
    up = lambda f: jnp.minimum(f, nf - 1)
    cv = lambda f: jnp.clip(f - 1, 0, nf - 1)
    dn = lambda f: jnp.clip(f - 2, 0, nf - 1)
    return pl.pallas_call(
        functools.partial(_ffn_kernel, tm=tm, tn=tn, nf=nf, final=final),
        grid=(s_len // tm, nf + 2),
        in_specs=[
            pl.BlockSpec((tm, d), lambda i, f: (i, 0), pipeline_mode=pl.Buffered(1)),
            _const_spec((1, d)),
            pl.BlockSpec((d, tf), lambda i, f: (0, up(f))),
            pl.BlockSpec((d, tf), lambda i, f: (0, nf + up(f))),
            pl.BlockSpec((CONV_WIDTH, tf), lambda i, f: (0, cv(f))),
            pl.BlockSpec((CONV_WIDTH, tf), lambda i, f: (0, nf + cv(f))),
            pl.BlockSpec((1, tf), lambda i, f: (0, cv(f))),
            pl.BlockSpec((1, tf), lambda i, f: (0, nf + cv(f))),
            pl.BlockSpec((tf, d), lambda i, f: (dn(f), 0)),
            _const_spec((1, d)),
        ],
        out_specs=pl.BlockSpec((tm, d), lambda i, f: (i, 0)),
        out_shape=jax.ShapeDtypeStruct((s_len, d), F32),
        scratch_shapes=[
            pltpu.VMEM((tm, d), BF16),
            pltpu.VMEM((tm, tf), BF16),
            pltpu.VMEM((tm, tf), BF16),
            pltpu.VMEM((tm, tf), F32),
            pltpu.VMEM((tm, tf), F32),
            pltpu.VMEM((tm, tf), F32),
            pltpu.VMEM((tm, tf), F32),
            pltpu.VMEM((nf, CONV_HALO, tf), F32),
            pltpu.VMEM((nf, CONV_HALO, tf), F32),
        ],
        compiler_params=_params("arbitrary", "arbitrary"),
        name="conv_ffn",
    )(h, norm_g.reshape(1, d), w_in_bf16, w_in_bf16, conv_w, conv_w, conv_b2, conv_b2, w_out_bf16,
      final_g.reshape(1, d))


def _kv_kernel(h_ref, g_ref, wd_ref, lat_ref, cos_ref, sin_ref, wuk_ref, wuvt_ref,
               k_ref, vt_ref, c_s, ct_s, kr_s, *, lora, rope, nope, hg):
    gidx = pl.program_id(1)

    @pl.when(gidx == 0)
    def _():
        hn = _rms(h_ref[...], g_ref[...]).astype(BF16)
        ckv = jnp.dot(hn, wd_ref[...], preferred_element_type=F32)
        c = _rms(ckv[:, :lora], lat_ref[...])
        c_s[...] = c.astype(BF16)
        ct_s[...] = c.T.astype(BF16)
        kr = ckv[:, lora:lora + rope] * cos_ref[...] + ckv[:, lora + rope:] * sin_ref[...]
        kr_s[...] = kr.astype(BF16)

    kn = jnp.dot(c_s[...], wuk_ref[...], preferred_element_type=F32)
    vt = jnp.dot(wuvt_ref[...], ct_s[...], preferred_element_type=F32)
    vd = vt.shape[0] // hg
    tm = kn.shape[0]
    hd = nope + rope
    k_pad = k_ref.shape[2] - hd
    ones_rows = vt_ref.shape[1] - vd
    for hh in range(hg):
        k_ref[hh, :, 0:nope] = kn[:, hh * nope:(hh + 1) * nope].astype(BF16)
        k_ref[hh, :, nope:hd] = kr_s[...]
        if k_pad:
            k_ref[hh, :, hd:hd + k_pad] = jnp.zeros((tm, k_pad), BF16)
        vt_ref[hh, 0:vd] = vt[hh * vd:(hh + 1) * vd].astype(BF16)
        vt_ref[hh, vd:vd + ones_rows] = jnp.ones((ones_rows, tm), BF16)


def _kv_proj(h, norm_g, wd_ext_bf16, lat_g, cos2, sin2, wuk_bf16, wuvt_bf16, *, n_heads, nope, rope,
             hd_pad, ones_rows, tm=512, hg=8):
    s_len, d = h.shape
    lora = wuk_bf16.shape[0]
    vd = wuvt_bf16.shape[0] // n_heads
    hg = min(hg, n_heads)
    assert s_len % tm == 0 and n_heads % hg == 0 and wd_ext_bf16.shape == (d, lora + 2 * rope)
    return pl.pallas_call(
        functools.partial(_kv_kernel, lora=lora, rope=rope, nope=nope, hg=hg),
        grid=(s_len // tm, n_heads // hg),
        in_specs=[
            pl.BlockSpec((tm, d), lambda i, g: (i, 0)),
            _const_spec((1, d)),
            _const_spec((d, lora + 2 * rope)),
            _const_spec((1, lora)),
            pl.BlockSpec((tm, rope), lambda i, g: (i, 0)),
            pl.BlockSpec((tm, rope), lambda i, g: (i, 0)),
            pl.BlockSpec((lora, hg * nope), lambda i, g: (0, g)),
            pl.BlockSpec((hg * vd, lora), lambda i, g: (g, 0)),
        ],
        out_specs=[
            pl.BlockSpec((hg, tm, hd_pad), lambda i, g: (g, i, 0)),
            pl.BlockSpec((hg, vd + ones_rows, tm), lambda i, g: (g, 0, i)),
        ],
        out_shape=[
            jax.ShapeDtypeStruct((n_heads, s_len, hd_pad), BF16),
            jax.ShapeDtypeStruct((n_heads, vd + ones_rows, s_len), BF16),
        ],
        scratch_shapes=[
            pltpu.VMEM((tm, lora), BF16),
            pltpu.VMEM((lora, tm), BF16),
            pltpu.VMEM((tm, rope), BF16),
        ],
        compiler_params=_params("parallel", "arbitrary"),
        name="kv_proj",
    )(h, norm_g.reshape(1, d), wd_ext_bf16, lat_g.reshape(1, lora), cos2, sin2, wuk_bf16, wuvt_bf16)


def _q_kernel(h_ref, g_ref, wdt_ref, lat_ref, cos_ref, sin_ref, wut_ref, qt_ref, cqt_s,
              *, nope, rope, hg, q_scale):
    gidx = pl.program_id(1)

    @pl.when(gidx == 0)
    def _():
        xn = _rms(h_ref[...], g_ref[...]).astype(BF16)
        cqt = lax.dot_general(wdt_ref[...], xn, (((1,), (1,)), ((), ())),
                              preferred_element_type=F32)
        ms = jnp.mean(cqt * cqt, axis=0, keepdims=True)
        cqt_s[...] = (cqt * lax.rsqrt(ms + EPS) * lat_ref[...]).astype(BF16)

    qt = jnp.dot(wut_ref[...], cqt_s[...], preferred_element_type=F32) * q_scale
    cos_t = cos_ref[...]
    sin_t = sin_ref[...]
    hd = nope + rope
    half = rope // 2
    pad = qt_ref.shape[1] - hd
    for hh in range(hg):
        b = hh * hd
        qt_ref[hh, 0:nope] = qt[b:b + nope].astype(BF16)
        x1 = qt[b + nope:b + nope + half]
        x2 = qt[b + nope + half:b + hd]
        qt_ref[hh, nope:nope + half] = (x1 * cos_t - x2 * sin_t).astype(BF16)
        qt_ref[hh, nope + half:hd] = (x2 * cos_t + x1 * sin_t).astype(BF16)
        if pad:
            qt_ref[hh, hd:hd + pad] = jnp.zeros((pad, qt.shape[1]), BF16)


def _q_proj(h, norm_g, wdqt_bf16, lat_g, cos_t, sin_t, wuqt_bf16, *, n_heads, nope, rope, hd_pad,
            q_scale, tm=512, hg=8):
    s_len, d = h.shape
    qlora = wdqt_bf16.shape[0]
    hd = nope + rope
    hg = min(hg, n_heads)
    assert s_len % tm == 0 and n_heads % hg == 0 and wuqt_bf16.shape == (n_heads * hd, qlora)
    return pl.pallas_call(
        functools.partial(_q_kernel, nope=nope, rope=rope, hg=hg, q_scale=q_scale),
        grid=(s_len // tm, n_heads // hg),
        in_specs=[
            pl.BlockSpec((tm, d), lambda i, g: (i, 0)),
            _const_spec((1, d)),
            _const_spec((qlora, d)),
            _const_spec((qlora, 1)),
            pl.BlockSpec((rope // 2, tm), lambda i, g: (0, i)),
            pl.BlockSpec((rope // 2, tm), lambda i, g: (0, i)),
            pl.BlockSpec((hg * hd, qlora), lambda i, g: (g, 0)),
        ],
        out_specs=pl.BlockSpec((hg, hd_pad, tm), lambda i, g: (g, 0, i)),
        out_shape=jax.ShapeDtypeStruct((n_heads, hd_pad, s_len), BF16),
        scratch_shapes=[pltpu.VMEM((qlora, tm), BF16)],
        compiler_params=_params("parallel", "arbitrary"),
        name="q_proj",
    )(h, norm_g.reshape(1, d), wdqt_bf16, lat_g.reshape(qlora, 1), cos_t, sin_t, wuqt_bf16)


def _attn_kernel(qt_ref, k_ref, vt_ref, *rest, tq, tk, vd, n_cast):
    w_f32 = rest[:n_cast]
    o_ref = rest[n_cast]
    w_bf16 = rest[n_cast + 1:2 * n_cast + 1]
    s0_ref, s1_ref, cmax0_ref, cmax1_ref, acc_ref, m_ref = rest[2 * n_cast + 1:]
    cmax_buf = (cmax0_ref, cmax1_ref)
    for src, dst in zip(w_f32, w_bf16):
        dst[...] = src[0].astype(BF16)

    i = pl.program_id(1)
    r = tq // tk
    s_buf = (s0_ref, s1_ref)

    def scores(j, slot, lo):
        kblk = k_ref[0, pl.ds(pl.multiple_of(j * tk, tk), tk), :]
        s = jnp.dot(kblk, qt_ref[0, :, lo:], preferred_element_type=F32)
        s_buf[slot][:, lo:] = s
        cmax_buf[slot][:, lo:] = jnp.max(s, axis=0, keepdims=True)

    def stage(j, slot, lo=0, diagonal=False, next_lo=0):
        if next_lo is not None:
            scores(j + 1, 1 - slot, next_lo)
        s = s_buf[slot][:, lo:]
        if diagonal:
            kpos = lax.broadcasted_iota(jnp.int32, s.shape, 0)
            qpos = lax.broadcasted_iota(jnp.int32, s.shape, 1)
            s = jnp.where(kpos <= qpos, s, MASK_VALUE)
            cmax = jnp.max(s, axis=0, keepdims=True)
        else:
            cmax = cmax_buf[slot][:, lo:]
        m = m_ref[:, lo:]
        m_new = jnp.maximum(m, cmax)
        alpha = jnp.exp2(m - m_new)
        p = jnp.exp2(s - m_new)
        m_ref[:, lo:] = m_new
        vt = vt_ref[0, :, pl.ds(pl.multiple_of(j * tk, tk), tk)]
        acc_ref[:, lo:] = alpha * acc_ref[:, lo:] + jnp.dot(vt, p.astype(BF16),
                                                           preferred_element_type=F32)

    m_ref[...] = jnp.full(m_ref.shape, MASK_VALUE, F32)
    acc_ref[...] = jnp.zeros(acc_ref.shape, F32)
    scores(0, 0, 0)

    def group(g, carry):
        for c in range(r):
            stage(g * r + c, c % 2)
        return carry

    lax.fori_loop(0, i, group, 0)
    for c in range(r):
        stage(i * r + c, c % 2, lo=c * tk, diagonal=True, next_lo=(c + 1) * tk if c < r - 1 else None)
    o_ref[...] = (acc_ref[0:vd] / acc_ref[vd:vd + 1]).T.astype(o_ref.dtype)


def _cast_block_rows(n_rows, n_steps):
    for rb in range(BF16_SUBLANES, n_rows + 1, BF16_SUBLANES):
        if n_rows % rb == 0 and n_rows // rb <= n_steps:
            return rb
    return None


def _attention(qt, k, vt, stacked_weights, *, vd, tq=1024, tk=256):
    n_heads, hd, s_len = qt.shape
    vda = vt.shape[1]
    tq = min(tq, s_len)
    assert k.shape == (n_heads, s_len, hd) and vt.shape == (n_heads, vda, s_len) and vda > vd
    assert s_len % tq == 0 and tq % (2 * tk) == 0
    nq = s_len // tq
    cast_in, cast_out, cast_shapes = [], [], []
    for w, layer in stacked_weights:
        _, n_rows, n_cols = w.shape
        rb = _cast_block_rows(n_rows, n_heads * nq)
        assert rb is not None
        blk = lambda h, i, last=n_rows // rb - 1: jnp.minimum(h * nq + i, last)
        cast_in.append(pl.BlockSpec((1, rb, n_cols), lambda h, i, blk=blk, layer=layer: (layer, blk(h, i), 0)))
        cast_out.append(pl.BlockSpec((rb, n_cols), lambda h, i, blk=blk: (blk(h, i), 0)))
        cast_shapes.append(jax.ShapeDtypeStruct((n_rows, n_cols), BF16))
    outs = pl.pallas_call(
        functools.partial(_attn_kernel, tq=tq, tk=tk, vd=vd, n_cast=len(stacked_weights)),
        grid=(n_heads, nq),
        in_specs=[
            pl.BlockSpec((1, hd, tq), lambda h, i: (h, 0, i)),
            pl.BlockSpec((1, s_len, hd), lambda h, i: (h, 0, 0)),
            pl.BlockSpec((1, vda, s_len), lambda h, i: (h, 0, 0)),
        ] + cast_in,
        out_specs=[pl.BlockSpec((tq, vd), lambda h, i: (i, h))] + cast_out,
        out_shape=[jax.ShapeDtypeStruct((s_len, n_heads * vd), BF16)] + cast_shapes,
        scratch_shapes=[
            pltpu.VMEM((tk, tq), F32),
            pltpu.VMEM((tk, tq), F32),
            pltpu.VMEM((1, tq), F32),
            pltpu.VMEM((1, tq), F32),
            pltpu.VMEM((vda, tq), F32),
            pltpu.VMEM((1, tq), F32),
        ],
        compiler_params=_params("arbitrary", "arbitrary"),
        name="mla_attention",
    )(qt, k, vt, *[w for w, _ in stacked_weights])
    return outs[0], outs[1:]


def _oproj_kernel(a_ref, w_ref, r_ref, o_ref):
    o_ref[...] = r_ref[...] + jnp.dot(a_ref[...], w_ref[...], preferred_element_type=F32)


def _out_proj(a_bf16, w_bf16, resid, *, tm=512, tn=1024):
    s_len, kdim = a_bf16.shape
    d = w_bf16.shape[1]
    tn = min(tn, d)
    assert s_len % tm == 0 and d % tn == 0
    return pl.pallas_call(
        _oproj_kernel,
        grid=(s_len // tm, d // tn),
        in_specs=[
            pl.BlockSpec((tm, kdim), lambda i, j: (i, 0)),
            pl.BlockSpec((kdim, tn), lambda i, j: (0, j)),
            pl.BlockSpec((tm, tn), lambda i, j: (i, j)),
        ],
        out_specs=pl.BlockSpec((tm, tn), lambda i, j: (i, j)),
        out_shape=jax.ShapeDtypeStruct((s_len, d), F32),
        compiler_params=_params("parallel", "parallel"),
        name="attn_out_proj",
    )(a_bf16, w_bf16, resid)


def _trunk(x, a_norm, a_pool_w, a_scale, kv_norm, w_dkv, kv_lat_norm, w_ukv, b_norm, w_dq, q_lat_norm,
           w_uq, w_o, ffn_norm, ffn_w_in, ffn_conv_w, ffn_conv_b, ffn_w_out, final_norm,
           *, pool_ts=256, ffn_tm=512, ffn_tf=256, proj_tm=512, attn_tq=2048, attn_tk=512,
           oproj_tm=512):
    batch, s_len, d = x.shape
    depth = ffn_norm.shape[0]
    n_a = a_norm.shape[0]
    n_b = b_norm.shape[0]
    assert n_a + n_b == depth and n_a >= 1
    lora, n_heads, kvd = w_ukv.shape
    hd = w_uq.shape[-1]
    rope = w_dkv.shape[1] - lora
    nope = hd - rope
    vd = kvd - nope
    half = rope // 2
    hd_pad = -(-hd // LANES) * LANES

    inv_freq = ROPE_THETA ** (-jnp.arange(0, rope, 2, dtype=F32) / rope)
    ang = jnp.arange(s_len, dtype=F32)[:, None] * inv_freq[None, :]
    cos, sin = jnp.cos(ang), jnp.sin(ang)
    cos2 = jnp.concatenate([cos, cos], axis=1)
    sin2 = jnp.concatenate([-sin, sin], axis=1)
    cos_t, sin_t = cos.T, sin.T

    bf = lambda w: w.astype(BF16)
    wd_ext = bf(jnp.concatenate([w_dkv, w_dkv[:, lora + half:], w_dkv[:, lora:lora + half]], axis=1))
    wuk = bf(w_ukv[:, :, :nope].reshape(lora, n_heads * nope))
    wuvt = bf(w_ukv[:, :, nope:].reshape(lora, n_heads * vd).T)
    final_g = final_norm

    outs = []
    ffn_bf16 = {}
    wo_bf16 = {}
    for b in range(batch):
        h = x.reshape(s_len, d) if batch == 1 else x[b]
        shared = None
        for layer in range(depth):
            last = layer == depth - 1
            if layer < n_a:
                h = _pool_layer(h, a_norm[layer], bf(a_pool_w[layer]), a_scale[layer], ts=pool_ts)
            else:
                j = layer - n_a
                qlora = w_dq.shape[2]
                qt = _q_proj(h, b_norm[j], bf(w_dq[j].T), q_lat_norm[j], cos_t, sin_t,
                             bf(w_uq[j].reshape(qlora, n_heads * hd).T),
                             n_heads=n_heads, nope=nope, rope=rope, hd_pad=hd_pad,
                             q_scale=hd ** -0.5 * math.log2(math.e), tm=proj_tm)
                first = layer not in ffn_bf16
                to_cast = ((ffn_w_in, layer), (ffn_w_out, layer), (w_o, j)) if first else ()
                o, cast = _attention(qt, shared[0], shared[1], to_cast, vd=vd, tq=attn_tq, tk=attn_tk)
                if first:
                    ffn_bf16[layer] = cast[:2]
                    wo_bf16[j] = cast[2]
                h = _out_proj(o, wo_bf16[j], h, tm=oproj_tm)
            if layer not in ffn_bf16:
                ffn_bf16[layer] = (bf(ffn_w_in[layer]), bf(ffn_w_out[layer]))
            h = _ffn_layer(h, ffn_norm[layer], ffn_bf16[layer][0], ffn_conv_w[layer], ffn_conv_b[layer],
                           ffn_bf16[layer][1], final_g, final=last, tm=ffn_tm, tf=ffn_tf)
            if layer == n_a - 1:
                shared = _kv_proj(h, kv_norm, wd_ext, kv_lat_norm, cos2, sin2, wuk, wuvt,
                                  n_heads=n_heads, nope=nope, rope=rope, hd_pad=hd_pad,
                                  ones_rows=BF16_SUBLANES, tm=proj_tm)
        outs.append(h)
    return outs[0].reshape(1, s_len, d) if batch == 1 else jnp.stack(outs, axis=0)


def kernel(x, a_norm, a_pool_w, a_scale, kv_norm, w_dkv, kv_lat_norm, w_ukv, b_norm, w_dq, q_lat_norm,
           w_uq, w_o, ffn_norm, ffn_w_in, ffn_conv_w, ffn_conv_b, ffn_w_out, final_norm):
    return _trunk(x, a_norm, a_pool_w, a_scale, kv_norm, w_dkv, kv_lat_norm, w_ukv, b_norm, w_dq,
                  q_lat_norm, w_uq, w_o, ffn_norm, ffn_w_in, ffn_conv_w, ffn_conv_b, ffn_w_out,
                  final_norm)
```

```python
import functools
import math

import jax
import jax.numpy as jnp
from jax import lax
from jax.experimental import pallas as pl
from jax.experimental.pallas import tpu as pltpu

EPS = 1e-6
ROPE_THETA = 10000.0
POOL_WINDOWS = (2, 4, 8, 16)
CONV_WIDTH = 3
POOL_HALO = 16
CONV_HALO = 8
CONV_PIECES = 16
MASK_VALUE = -1e30
VMEM_LIMIT_BYTES = 56 * 1024 * 1024
LANES = 128
BF16_SUBLANES = 16

F32 = jnp.float32
BF16 = jnp.bfloat16


def _rms(x, g):
    return x * lax.rsqrt(jnp.mean(x * x, axis=-1, keepdims=True) + EPS) * g


def _params(*sem):
    return pltpu.CompilerParams(dimension_semantics=sem, vmem_limit_bytes=VMEM_LIMIT_BYTES)


def _const_spec(shape):
    nd = len(shape)
    return pl.BlockSpec(shape, lambda *_: (0,) * nd, pipeline_mode=pl.Buffered(1))


def _pool_kernel(x_ref, halo_ref, g_ref, w_ref, sc_ref, o_ref, *, ts, gc):
    i = pl.program_id(0)
    g = g_ref[...]
    x = x_ref[...]
    xn = _rms(x, g)
    hn = jnp.where(i == 0, 0.0, _rms(halo_ref[...], g))
    pos = i * ts + lax.broadcasted_iota(jnp.int32, (ts, 1), 0)
    for gi, w in enumerate(POOL_WINDOWS):
        sl = slice(gi * gc, (gi + 1) * gc)
        xg = xn[:, sl]
        s = jnp.concatenate([hn[:, sl], xg], axis=0)
        k = 1
        while k < w:
            s = s + pltpu.roll(s, k, axis=0)
            k *= 2
        cnt = jnp.minimum(pos + 1, w).astype(F32)
        pooled = s[POOL_HALO:] / cnt - xg
        mixed = jnp.dot(pooled.astype(BF16), w_ref[gi], preferred_element_type=F32)
        o_ref[:, sl] = x[:, sl] + mixed * sc_ref[:, sl]


def _pool_layer(x, norm_g, w_bf16, scale, *, ts=256):
    s_len, d = x.shape
    ng, gc, _ = w_bf16.shape
    assert ng == len(POOL_WINDOWS) and ng * gc == d and s_len % ts == 0 and ts % POOL_HALO == 0
    per = ts // POOL_HALO
    return pl.pallas_call(
        functools.partial(_pool_kernel, ts=ts, gc=gc),
        grid=(s_len // ts,),
        in_specs=[
            pl.BlockSpec((ts, d), lambda i: (i, 0)),
            pl.BlockSpec((POOL_HALO, d), lambda i: (jnp.maximum(i * per - 1, 0), 0)),
            _const_spec((1, d)),
            _const_spec((ng, gc, gc)),
            _const_spec((1, d)),
        ],
        out_specs=pl.BlockSpec((ts, d), lambda i: (i, 0)),
        out_shape=jax.ShapeDtypeStruct((s_len, d), F32),
        compiler_params=_params("parallel"),
        name="pool_layer",
    )(x, x, norm_g.reshape(1, d), w_bf16, scale.reshape(1, d))


def _ffn_kernel(h_ref, g_ref, wg_ref, wv_ref, cwg_ref, cwv_ref, bg_ref, bv_ref, wo_ref, fg_ref,
                o_ref, xn_s, a0_s, a1_s, ug0_s, ug1_s, uv0_s, uv1_s, cg_s, cv_s, *, tm, tn, nf, final):
    i = pl.program_id(0)
    f = pl.program_id(1)
    acts = (a0_s, a1_s)
    ugs = (ug0_s, ug1_s)
    uvs = (uv0_s, uv1_s)

    @pl.when(f == 0)
    def _():
        h = h_ref[...]
        xn_s[...] = _rms(h, g_ref[...]).astype(BF16)
        o_ref[...] = h

    c = jnp.clip(f - 1, 0, nf - 1)

    @pl.when((i == 0) & (f >= 1) & (f <= nf))
    def _():
        cg_s[c] = jnp.zeros(cg_s.shape[1:], F32)
        cv_s[c] = jnp.zeros(cv_s.shape[1:], F32)

    n_down = o_ref.shape[1] // tn
    rows = tm // CONV_PIECES

    def up_piece(p, which):
        u_s, w_ref = ((ugs[p], wg_ref), (uvs[p], wv_ref))[which]
        u_s[...] = jnp.dot(xn_s[...], w_ref[...], preferred_element_type=F32)

    def conv_rows(u_s, cw_ref, b_ref, c_s, r0):
        prev = c_s[c] if r0 == 0 else u_s[r0 - CONV_HALO:r0]
        u = u_s[r0:r0 + rows]
        cat = jnp.concatenate([prev, u], axis=0)
        u1 = pltpu.roll(cat, 1, axis=0)[CONV_HALO:]
        u2 = pltpu.roll(cat, 2, axis=0)[CONV_HALO:]
        cw = cw_ref[...]
        return cw[0:1] * u2 + cw[1:2] * u1 + cw[2:3] * u + b_ref[...]

    def conv_piece(p, k):
        r0 = k * rows
        gate = conv_rows(ugs[p], cwg_ref, bg_ref, cg_s, r0)
        val = conv_rows(uvs[p], cwv_ref, bv_ref, cv_s, r0)
        acts[p][r0:r0 + rows] = (gate * (1.0 / (1.0 + jnp.exp(-gate))) * val).astype(BF16)
        if k == CONV_PIECES - 1:
            cg_s[c] = ugs[p][tm - CONV_HALO:]
            cv_s[c] = uvs[p][tm - CONV_HALO:]

    def down_piece(p, n):
        sl = slice(n * tn, (n + 1) * tn)
        o_ref[:, sl] += jnp.dot(acts[p][...], wo_ref[:, sl], preferred_element_type=F32)

    def run(p_up=None, p_conv=None, p_down=None):
        mxu, vpu = [], []
        if p_up is not None:
            mxu += [functools.partial(up_piece, p_up, w) for w in range(2)]
        if p_down is not None:
            mxu += [functools.partial(down_piece, p_down, n) for n in range(n_down)]
        if p_conv is not None:
            vpu += [functools.partial(conv_piece, p_conv, k) for k in range(CONV_PIECES)]
        for idx in range(max(len(mxu), len(vpu))):
            if idx < len(mxu):
                mxu[idx]()
            if idx < len(vpu):
                vpu[idx]()

    @pl.when(f == 0)
    def _():
        run(p_up=0)

    @pl.when(f == 1)
    def _():
        run(p_up=1, p_conv=0)

    for p in range(2):
        @pl.when((f >= 2) & (f < nf) & (f % 2 == p))
        def _():
            run(p_up=p, p_conv=1 - p, p_down=p)

    @pl.when(f == nf)
    def _():
        run(p_conv=(nf - 1) % 2, p_down=nf % 2)

    @pl.when(f == nf + 1)
    def _():
        run(p_down=(nf - 1) % 2)
        if final:
            o_ref[...] = _rms(o_ref[...], fg_ref[...])


def _ffn_layer(h, norm_g, w_in_bf16, conv_w, conv_b, w_out_bf16, final_g, *, final, tm=512, tf=256,
               tn=256):
    s_len, d = h.shape
    dff = w_out_bf16.shape[0]
    tn = min(tn, d)
    assert s_len % tm == 0 and dff % tf == 0 and d % tn == 0 and w_in_bf16.shape == (d, 2 * dff)
    nf = dff // tf
    assert nf >= 2
    conv_b2 = conv_b.reshape(1, 2 * dff)
    up = lambda f: jnp.minimum(f, nf - 1)
    cv = lambda f: jnp.clip(f - 1, 0, nf - 1)
    dn = lambda f: jnp.clip(f - 2, 0, nf - 1)
    return pl.pallas_call(
        functools.partial(_ffn_kernel, tm=tm, tn=tn, nf=nf, final=final),
        grid=(s_len // tm, nf + 2),
        in_specs=[
            pl.BlockSpec((tm, d), lambda i, f: (i, 0), pipeline_mode=pl.Buffered(1)),
            _const_spec((1, d)),
            pl.BlockSpec((d, tf), lambda i, f: (0, up(f))),
            pl.BlockSpec((d, tf), lambda i, f: (0, nf + up(f))),
            pl.BlockSpec((CONV_WIDTH, tf), lambda i, f: (0, cv(f))),
            pl.BlockSpec((CONV_WIDTH, tf), lambda i, f: (0, nf + cv(f))),
            pl.BlockSpec((1, tf), lambda i, f: (0, cv(f))),
            pl.BlockSpec((1, tf), lambda i, f: (0, nf + cv(f))),
            pl.BlockSpec((tf, d), lambda i, f: (dn(f), 0)),
            _const_spec((1, d)),
        ],
        out_specs=pl.BlockSpec((tm, d), lambda i, f: (i, 0)),
        out_shape=jax.ShapeDtypeStruct((s_len, d), F32),
        scratch_shapes=[
            pltpu.VMEM((tm, d), BF16),
            pltpu.VMEM((tm, tf), BF16),
            pltpu.VMEM((tm, tf), BF16),
            pltpu.VMEM((tm, tf), F32),
            pltpu.VMEM((tm, tf), F32),
            pltpu.VMEM((tm, tf), F32),
            pltpu.VMEM((tm, tf), F32),
            pltpu.VMEM((nf, CONV_HALO, tf), F32),
            pltpu.VMEM((nf, CONV_HALO, tf), F32),
        ],
        compiler_params=_params("arbitrary", "arbitrary"),
        name="conv_ffn",
    )(h, norm_g.reshape(1, d), w_in_bf16, w_in_bf16, conv_w, conv_w, conv_b2, conv_b2, w_out_bf16,
      final_g.reshape(1, d))


def _kv_kernel(h_ref, g_ref, wd_ref, lat_ref, cos_ref, sin_ref, wuk_ref, wuvt_ref,
               k_ref, vt_ref, c_s, ct_s, kr_s, *, lora, rope, nope, hg):
    gidx = pl.program_id(1)

    @pl.when(gidx == 0)
    def _():
        hn = _rms(h_ref[...], g_ref[...]).astype(BF16)
        ckv = jnp.dot(hn, wd_ref[...], preferred_element_type=F32)
        c = _rms(ckv[:, :lora], lat_ref[...])
        c_s[...] = c.astype(BF16)
        ct_s[...] = c.T.astype(BF16)
        kr = ckv[:, lora:lora + rope] * cos_ref[...] + ckv[:, lora + rope:] * sin_ref[...]
        kr_s[...] = kr.astype(BF16)

    kn = jnp.dot(c_s[...], wuk_ref[...], preferred_element_type=F32)
    vt = jnp.dot(wuvt_ref[...], ct_s[...], preferred_element_type=F32)
    vd = vt.shape[0] // hg
    tm = kn.shape[0]
    hd = nope + rope
    k_pad = k_ref.shape[2] - hd
    ones_rows = vt_ref.shape[1] - vd
    for hh in range(hg):
        k_ref[hh, :, 0:nope] = kn[:, hh * nope:(hh + 1) * nope].astype(BF16)
        k_ref[hh, :, nope:hd] = kr_s[...]
        if k_pad:
            k_ref[hh, :, hd:hd + k_pad] = jnp.zeros((tm, k_pad), BF16)
        vt_ref[hh, 0:vd] = vt[hh * vd:(hh + 1) * vd].astype(BF16)
        vt_ref[hh, vd:vd + ones_rows] = jnp.ones((ones_rows, tm), BF16)


def _kv_proj(h, norm_g, wd_ext_bf16, lat_g, cos2, sin2, wuk_bf16, wuvt_bf16, *, n_heads, nope, rope,
             hd_pad, ones_rows, tm=512, hg=16):
    s_len, d = h.shape
    lora = wuk_bf16.shape[0]
    vd = wuvt_bf16.shape[0] // n_heads
    hg = min(hg, n_heads)
    assert s_len % tm == 0 and n_heads % hg == 0 and wd_ext_bf16.shape == (d, lora + 2 * rope)
    return pl.pallas_call(
        functools.partial(_kv_kernel, lora=lora, rope=rope, nope=nope, hg=hg),
        grid=(s_len // tm, n_heads // hg),
        in_specs=[
            pl.BlockSpec((tm, d), lambda i, g: (i, 0)),
            _const_spec((1, d)),
            _const_spec((d, lora + 2 * rope)),
            _const_spec((1, lora)),
            pl.BlockSpec((tm, rope), lambda i, g: (i, 0)),
            pl.BlockSpec((tm, rope), lambda i, g: (i, 0)),
            pl.BlockSpec((lora, hg * nope), lambda i, g: (0, g)),
            pl.BlockSpec((hg * vd, lora), lambda i, g: (g, 0)),
        ],
        out_specs=[
            pl.BlockSpec((hg, tm, hd_pad), lambda i, g: (g, i, 0)),
            pl.BlockSpec((hg, vd + ones_rows, tm), lambda i, g: (g, 0, i)),
        ],
        out_shape=[
            jax.ShapeDtypeStruct((n_heads, s_len, hd_pad), BF16),
            jax.ShapeDtypeStruct((n_heads, vd + ones_rows, s_len), BF16),
        ],
        scratch_shapes=[
            pltpu.VMEM((tm, lora), BF16),
            pltpu.VMEM((lora, tm), BF16),
            pltpu.VMEM((tm, rope), BF16),
        ],
        compiler_params=_params("parallel", "arbitrary"),
        name="kv_proj",
    )(h, norm_g.reshape(1, d), wd_ext_bf16, lat_g.reshape(1, lora), cos2, sin2, wuk_bf16, wuvt_bf16)


def _q_kernel(h_ref, g_ref, wdt_ref, lat_ref, cos_ref, sin_ref, wut_ref, qt_ref, cqt_s,
              *, nope, rope, hg, q_scale):
    gidx = pl.program_id(1)

    @pl.when(gidx == 0)
    def _():
        xn = _rms(h_ref[...], g_ref[...]).astype(BF16)
        cqt = lax.dot_general(wdt_ref[...], xn, (((1,), (1,)), ((), ())),
                              preferred_element_type=F32)
        ms = jnp.mean(cqt * cqt, axis=0, keepdims=True)
        cqt_s[...] = (cqt * lax.rsqrt(ms + EPS) * lat_ref[...]).astype(BF16)

    qt = jnp.dot(wut_ref[...], cqt_s[...], preferred_element_type=F32) * q_scale
    cos_t = cos_ref[...]
    sin_t = sin_ref[...]
    hd = nope + rope
    half = rope // 2
    pad = qt_ref.shape[1] - hd
    for hh in range(hg):
        b = hh * hd
        qt_ref[hh, 0:nope] = qt[b:b + nope].astype(BF16)
        x1 = qt[b + nope:b + nope + half]
        x2 = qt[b + nope + half:b + hd]
        qt_ref[hh, nope:nope + half] = (x1 * cos_t - x2 * sin_t).astype(BF16)
        qt_ref[hh, nope + half:hd] = (x2 * cos_t + x1 * sin_t).astype(BF16)
        if pad:
            qt_ref[hh, hd:hd + pad] = jnp.zeros((pad, qt.shape[1]), BF16)


def _q_proj(h, norm_g, wdqt_bf16, lat_g, cos_t, sin_t, wuqt_bf16, *, n_heads, nope, rope, hd_pad,
            q_scale, tm=512, hg=16):
    s_len, d = h.shape
    qlora = wdqt_bf16.shape[0]
    hd = nope + rope
    hg = min(hg, n_heads)
    assert s_len % tm == 0 and n_heads % hg == 0 and wuqt_bf16.shape == (n_heads * hd, qlora)
    return pl.pallas_call(
        functools.partial(_q_kernel, nope=nope, rope=rope, hg=hg, q_scale=q_scale),
        grid=(s_len // tm, n_heads // hg),
        in_specs=[
            pl.BlockSpec((tm, d), lambda i, g: (i, 0)),
            _const_spec((1, d)),
            _const_spec((qlora, d)),
            _const_spec((qlora, 1)),
            pl.BlockSpec((rope // 2, tm), lambda i, g: (0, i)),
            pl.BlockSpec((rope // 2, tm), lambda i, g: (0, i)),
            pl.BlockSpec((hg * hd, qlora), lambda i, g: (g, 0)),
        ],
        out_specs=pl.BlockSpec((hg, hd_pad, tm), lambda i, g: (g, 0, i)),
        out_shape=jax.ShapeDtypeStruct((n_heads, hd_pad, s_len), BF16),
        scratch_shapes=[pltpu.VMEM((qlora, tm), BF16)],
        compiler_params=_params("parallel", "arbitrary"),
        name="q_proj",
    )(h, norm_g.reshape(1, d), wdqt_bf16, lat_g.reshape(qlora, 1), cos_t, sin_t, wuqt_bf16)


def _attn_kernel(qt_ref, k_ref, vt_ref, *rest, tq, tk, cb, vd, n_cast):
    w_f32 = rest[:n_cast]
    o_ref = rest[n_cast]
    w_bf16 = rest[n_cast + 1:2 * n_cast + 1]
    s0_ref, s1_ref, cmax0_ref, cmax1_ref, acc_ref, m_ref = rest[2 * n_cast + 1:]
    cmax_buf = (cmax0_ref, cmax1_ref)
    for src, dst in zip(w_f32, w_bf16):
        dst[...] = src[0].astype(BF16)

    i = pl.program_id(1)
    r = tq // tk
    s_buf = (s0_ref, s1_ref)

    def scores(j, slot, c0, c1):
        kblk = k_ref[0, pl.ds(pl.multiple_of(j * tk, tk), tk), :]
        s = jnp.dot(kblk, qt_ref[0, :, c0:c1], preferred_element_type=F32)
        s_buf[slot][:, c0:c1] = s
        cmax_buf[slot][:, c0:c1] = jnp.max(s, axis=0, keepdims=True)

    def weights_values(j, slot, c0, c1, key_offset):
        s = s_buf[slot][:, c0:c1]
        if key_offset is not None and key_offset + tk - 1 > c0:
            kpos = lax.broadcasted_iota(jnp.int32, s.shape, 0) + key_offset
            qpos = lax.broadcasted_iota(jnp.int32, s.shape, 1) + c0
            s = jnp.where(kpos <= qpos, s, MASK_VALUE)
            cmax = jnp.max(s, axis=0, keepdims=True)
        else:
            cmax = cmax_buf[slot][:, c0:c1]
        m = m_ref[:, c0:c1]
        m_new = jnp.maximum(m, cmax)
        alpha = jnp.exp2(m - m_new)
        p = jnp.exp2(s - m_new)
        m_ref[:, c0:c1] = m_new
        vt = vt_ref[0, :, pl.ds(pl.multiple_of(j * tk, tk), tk)]
        acc_ref[:, c0:c1] = alpha * acc_ref[:, c0:c1] + jnp.dot(vt, p.astype(BF16),
                                                               preferred_element_type=F32)

    def col_blocks(lo):
        return [(c0, min(c0 + cb, tq)) for c0 in range(lo, tq, cb)]

    def stage(j, slot, lo=0, diagonal=False, next_lo=0):
        cur = col_blocks(lo)
        nxt = col_blocks(next_lo) if next_lo is not None else []
        for idx in range(max(len(cur), len(nxt))):
            if idx < len(nxt):
                scores(j + 1, 1 - slot, *nxt[idx])
            if idx < len(cur):
                weights_values(j, slot, *cur[idx], key_offset=lo if diagonal else None)

    m_ref[...] = jnp.full(m_ref.shape, MASK_VALUE, F32)
    acc_ref[...] = jnp.zeros(acc_ref.shape, F32)
    for c0, c1 in col_blocks(0):
        scores(0, 0, c0, c1)

    def group(g, carry):
        for c in range(r):
            stage(g * r + c, c % 2)
        return carry

    lax.fori_loop(0, i, group, 0)
    for c in range(r):
        stage(i * r + c, c % 2, lo=c * tk, diagonal=True, next_lo=(c + 1) * tk if c < r - 1 else None)
    o_ref[...] = (acc_ref[0:vd] / acc_ref[vd:vd + 1]).T.astype(o_ref.dtype)


def _cast_block_rows(n_rows, n_steps):
    for rb in range(BF16_SUBLANES, n_rows + 1, BF16_SUBLANES):
        if n_rows % rb == 0 and n_rows // rb <= n_steps:
            return rb
    return None


def _attention(qt, k, vt, stacked_weights, *, vd, tq=2048, tk=512, cb=512):
    n_heads, hd, s_len = qt.shape
    vda = vt.shape[1]
    tq = min(tq, s_len)
    assert k.shape == (n_heads, s_len, hd) and vt.shape == (n_heads, vda, s_len) and vda > vd
    assert s_len % tq == 0 and tq % (2 * tk) == 0
    nq = s_len // tq
    cast_in, cast_out, cast_shapes = [], [], []
    for w, layer in stacked_weights:
        _, n_rows, n_cols = w.shape
        rb = _cast_block_rows(n_rows, n_heads * nq)
        assert rb is not None
        blk = lambda h, i, last=n_rows // rb - 1: jnp.minimum(h * nq + i, last)
        cast_in.append(pl.BlockSpec((1, rb, n_cols), lambda h, i, blk=blk, layer=layer: (layer, blk(h, i), 0)))
        cast_out.append(pl.BlockSpec((rb, n_cols), lambda h, i, blk=blk: (blk(h, i), 0)))
        cast_shapes.append(jax.ShapeDtypeStruct((n_rows, n_cols), BF16))
    outs = pl.pallas_call(
        functools.partial(_attn_kernel, tq=tq, tk=tk, cb=min(cb, tq), vd=vd, n_cast=len(stacked_weights)),
        grid=(n_heads, nq),
        in_specs=[
            pl.BlockSpec((1, hd, tq), lambda h, i: (h, 0, i)),
            pl.BlockSpec((1, s_len, hd), lambda h, i: (h, 0, 0)),
            pl.BlockSpec((1, vda, s_len), lambda h, i: (h, 0, 0)),
        ] + cast_in,
        out_specs=[pl.BlockSpec((tq, vd), lambda h, i: (i, h))] + cast_out,
        out_shape=[jax.ShapeDtypeStruct((s_len, n_heads * vd), BF16)] + cast_shapes,
        scratch_shapes=[
            pltpu.VMEM((tk, tq), F32),
            pltpu.VMEM((tk, tq), F32),
            pltpu.VMEM((1, tq), F32),
            pltpu.VMEM((1, tq), F32),
            pltpu.VMEM((vda, tq), F32),
            pltpu.VMEM((1, tq), F32),
        ],
        compiler_params=_params("arbitrary", "arbitrary"),
        name="mla_attention",
    )(qt, k, vt, *[w for w, _ in stacked_weights])
    return outs[0], outs[1:]


def _oproj_kernel(a_ref, w_ref, r_ref, o_ref):
    o_ref[...] = r_ref[...] + jnp.dot(a_ref[...], w_ref[...], preferred_element_type=F32)


def _out_proj(a_bf16, w_bf16, resid, *, tm=512, tn=1024):
    s_len, kdim = a_bf16.shape
    d = w_bf16.shape[1]
    tn = min(tn, d)
    assert s_len % tm == 0 and d % tn == 0
    return pl.pallas_call(
        _oproj_kernel,
        grid=(s_len // tm, d // tn),
        in_specs=[
            pl.BlockSpec((tm, kdim), lambda i, j: (i, 0)),
            pl.BlockSpec((kdim, tn), lambda i, j: (0, j)),
            pl.BlockSpec((tm, tn), lambda i, j: (i, j)),
        ],
        out_specs=pl.BlockSpec((tm, tn), lambda i, j: (i, j)),
        out_shape=jax.ShapeDtypeStruct((s_len, d), F32),
        compiler_params=_params("parallel", "parallel"),
        name="attn_out_proj",
    )(a_bf16, w_bf16, resid)


def _trunk(x, a_norm, a_pool_w, a_scale, kv_norm, w_dkv, kv_lat_norm, w_ukv, b_norm, w_dq, q_lat_norm,
           w_uq, w_o, ffn_norm, ffn_w_in, ffn_conv_w, ffn_conv_b, ffn_w_out, final_norm,
           *, pool_ts=256, ffn_tm=512, ffn_tf=256, proj_tm=512, attn_tq=2048, attn_tk=512, attn_cb=256,
           oproj_tm=512):
    batch, s_len, d = x.shape
    depth = ffn_norm.shape[0]
    n_a = a_norm.shape[0]
    n_b = b_norm.shape[0]
    assert n_a + n_b == depth and n_a >= 1
    lora, n_heads, kvd = w_ukv.shape
    hd = w_uq.shape[-1]
    rope = w_dkv.shape[1] - lora
    nope = hd - rope
    vd = kvd - nope
    half = rope // 2
    hd_pad = -(-hd // LANES) * LANES

    inv_freq = ROPE_THETA ** (-jnp.arange(0, rope, 2, dtype=F32) / rope)
    ang = jnp.arange(s_len, dtype=F32)[:, None] * inv_freq[None, :]
    cos, sin = jnp.cos(ang), jnp.sin(ang)
    cos2 = jnp.concatenate([cos, cos], axis=1)
    sin2 = jnp.concatenate([-sin, sin], axis=1)
    cos_t, sin_t = cos.T, sin.T

    bf = lambda w: w.astype(BF16)
    wd_ext = bf(jnp.concatenate([w_dkv, w_dkv[:, lora + half:], w_dkv[:, lora:lora + half]], axis=1))
    wuk = bf(w_ukv[:, :, :nope].reshape(lora, n_heads * nope))
    wuvt = bf(w_ukv[:, :, nope:].reshape(lora, n_heads * vd).T)
    final_g = final_norm

    outs = []
    ffn_bf16 = {}
    wo_bf16 = {}
    for b in range(batch):
        h = x.reshape(s_len, d) if batch == 1 else x[b]
        shared = None
        for layer in range(depth):
            last = layer == depth - 1
            if layer < n_a:
                h = _pool_layer(h, a_norm[layer], bf(a_pool_w[layer]), a_scale[layer], ts=pool_ts)
            else:
                j = layer - n_a
                qlora = w_dq.shape[2]
                qt = _q_proj(h, b_norm[j], bf(w_dq[j].T), q_lat_norm[j], cos_t, sin_t,
                             bf(w_uq[j].reshape(qlora, n_heads * hd).T),
                             n_heads=n_heads, nope=nope, rope=rope, hd_pad=hd_pad,
                             q_scale=hd ** -0.5 * math.log2(math.e), tm=proj_tm)
                first = layer not in ffn_bf16
                to_cast = ((ffn_w_in, layer), (ffn_w_out, layer), (w_o, j)) if first else ()
                o, cast = _attention(qt, shared[0], shared[1], to_cast, vd=vd, tq=attn_tq, tk=attn_tk,
                                     cb=attn_cb)
                if first:
                    ffn_bf16[layer] = cast[:2]
                    wo_bf16[j] = cast[2]
                h = _out_proj(o, wo_bf16[j], h, tm=oproj_tm)
            if layer not in ffn_bf16:
                ffn_bf16[layer] = (bf(ffn_w_in[layer]), bf(ffn_w_out[layer]))
            h = _ffn_layer(h, ffn_norm[layer], ffn_bf16[layer][0], ffn_conv_w[layer], ffn_conv_b[layer],
                           ffn_bf16[layer][1], final_g, final=last, tm=ffn_tm, tf=ffn_tf)
            if layer == n_a - 1:
                shared = _kv_proj(h, kv_norm, wd_ext, kv_lat_norm, cos2, sin2, wuk, wuvt,
                                  n_heads=n_heads, nope=nope, rope=rope, hd_pad=hd_pad,
                                  ones_rows=BF16_SUBLANES, tm=proj_tm)
        outs.append(h)
    return outs[0].reshape(1, s_len, d) if batch == 1 else jnp.stack(outs, axis=0)


def kernel(x, a_norm, a_pool_w, a_scale, kv_norm, w_dkv, kv_lat_norm, w_ukv, b_norm, w_dq, q_lat_norm,
           w_uq, w_o, ffn_norm, ffn_w_in, ffn_conv_w, ffn_conv_b, ffn_w_out, final_norm):
    return _trunk(x, a_norm, a_pool_w, a_scale, kv_norm, w_dkv, kv_lat_norm, w_ukv, b_norm, w_dq,
                  q_lat_norm, w_uq, w_o, ffn_norm, ffn_w_in, ffn_conv_w, ffn_conv_b, ffn_w_out,
                  final_norm)
```

```python
import functools
import math

import jax
import jax.numpy as jnp
from jax import lax
from jax.experimental import pallas as pl
from jax.experimental.pallas import tpu as pltpu

EPS = 1e-6
ROPE_THETA = 10000.0
POOL_WINDOWS = (2, 4, 8, 16)
CONV_WIDTH = 3
POOL_HALO = 16
CONV_HALO = 8
NORM_PIECES = 4
CONV_PIECES = 16
MASK_VALUE = -1e30
VMEM_LIMIT_BYTES = 56 * 1024 * 1024
LANES = 128
BF16_SUBLANES = 16

F32 = jnp.float32
BF16 = jnp.bfloat16


def _rms(x, g):
    return x * lax.rsqrt(jnp.mean(x * x, axis=-1, keepdims=True) + EPS) * g


def _params(*sem):
    return pltpu.CompilerParams(dimension_semantics=sem, vmem_limit_bytes=VMEM_LIMIT_BYTES)


def _const_spec(shape):
    nd = len(shape)
    return pl.BlockSpec(shape, lambda *_: (0,) * nd, pipeline_mode=pl.Buffered(1))


def _pool_kernel(x_ref, halo_ref, g_ref, w_ref, sc_ref, o_ref, *, ts, gc):
    i = pl.program_id(0)
    g = g_ref[...]
    x = x_ref[...]
    xn = _rms(x, g)
    hn = jnp.where(i == 0, 0.0, _rms(halo_ref[...], g))
    pos = i * ts + lax.broadcasted_iota(jnp.int32, (ts, 1), 0)
    for gi, w in enumerate(POOL_WINDOWS):
        sl = slice(gi * gc, (gi + 1) * gc)
        xg = xn[:, sl]
        s = jnp.concatenate([hn[:, sl], xg], axis=0)
        k = 1
        while k < w:
            s = s + pltpu.roll(s, k, axis=0)
            k *= 2
        cnt = jnp.minimum(pos + 1, w).astype(F32)
        pooled = s[POOL_HALO:] / cnt - xg
        mixed = jnp.dot(pooled.astype(BF16), w_ref[gi], preferred_element_type=F32)
        o_ref[:, sl] = x[:, sl] + mixed * sc_ref[:, sl]


def _pool_layer(x, norm_g, w_bf16, scale, *, ts=256):
    s_len, d = x.shape
    ng, gc, _ = w_bf16.shape
    assert ng == len(POOL_WINDOWS) and ng * gc == d and s_len % ts == 0 and ts % POOL_HALO == 0
    per = ts // POOL_HALO
    return pl.pallas_call(
        functools.partial(_pool_kernel, ts=ts, gc=gc),
        grid=(s_len // ts,),
        in_specs=[
            pl.BlockSpec((ts, d), lambda i: (i, 0)),
            pl.BlockSpec((POOL_HALO, d), lambda i: (jnp.maximum(i * per - 1, 0), 0)),
            _const_spec((1, d)),
            _const_spec((ng, gc, gc)),
            _const_spec((1, d)),
        ],
        out_specs=pl.BlockSpec((ts, d), lambda i: (i, 0)),
        out_shape=jax.ShapeDtypeStruct((s_len, d), F32),
        compiler_params=_params("parallel"),
        name="pool_layer",
    )(x, x, norm_g.reshape(1, d), w_bf16, scale.reshape(1, d))


def _ffn_kernel(h_ref, g_ref, wg_ref, wv_ref, cwg_ref, cwv_ref, bg_ref, bv_ref, wo_ref, fg_ref,
                o_ref, xn_s, a0_s, a1_s, ug0_s, ug1_s, uv0_s, uv1_s, cg_s, cv_s, *, tm, tn, nf, final):
    i = pl.program_id(0)
    f = pl.program_id(1)
    acts = (a0_s, a1_s)
    ugs = (ug0_s, ug1_s)
    uvs = (uv0_s, uv1_s)

    c = jnp.clip(f - 1, 0, nf - 1)

    @pl.when((i == 0) & (f >= 1) & (f <= nf))
    def _():
        cg_s[c] = jnp.zeros(cg_s.shape[1:], F32)
        cv_s[c] = jnp.zeros(cv_s.shape[1:], F32)

    n_down = o_ref.shape[1] // tn
    rows = tm // CONV_PIECES

    def up_piece(p, which, half):
        u_s, w_ref = ((ugs[p], wg_ref), (uvs[p], wv_ref))[which]
        r = slice(half * (tm // 2), (half + 1) * (tm // 2))
        u_s[r] = jnp.dot(xn_s[r], w_ref[...], preferred_element_type=F32)

    def conv_rows(u_s, cw_ref, b_ref, c_s, r0):
        prev = c_s[c] if r0 == 0 else u_s[r0 - CONV_HALO:r0]
        u = u_s[r0:r0 + rows]
        cat = jnp.concatenate([prev, u], axis=0)
        u1 = pltpu.roll(cat, 1, axis=0)[CONV_HALO:]
        u2 = pltpu.roll(cat, 2, axis=0)[CONV_HALO:]
        cw = cw_ref[...]
        return cw[0:1] * u2 + cw[1:2] * u1 + cw[2:3] * u + b_ref[...]

    def conv_piece(p, k):
        r0 = k * rows
        gate = conv_rows(ugs[p], cwg_ref, bg_ref, cg_s, r0)
        val = conv_rows(uvs[p], cwv_ref, bv_ref, cv_s, r0)
        acts[p][r0:r0 + rows] = (gate * (1.0 / (1.0 + jnp.exp(-gate))) * val).astype(BF16)
        if k == CONV_PIECES - 1:
            cg_s[c] = ugs[p][tm - CONV_HALO:]
            cv_s[c] = uvs[p][tm - CONV_HALO:]

    def down_piece(p, n):
        sl = slice(n * tn, (n + 1) * tn)
        o_ref[:, sl] += jnp.dot(acts[p][...], wo_ref[:, sl], preferred_element_type=F32)

    def run(p_up=None, p_conv=None, p_down=None):
        mxu, vpu = [], []
        if p_up is not None:
            mxu += [functools.partial(up_piece, p_up, w, half) for half in range(2) for w in range(2)]
        if p_down is not None:
            mxu += [functools.partial(down_piece, p_down, n) for n in range(n_down)]
        if p_conv is not None:
            vpu += [functools.partial(conv_piece, p_conv, k) for k in range(CONV_PIECES)]
        for idx in range(max(len(mxu), len(vpu))):
            if idx < len(mxu):
                mxu[idx]()
            if idx < len(vpu):
                vpu[idx]()

    @pl.when(f == 0)
    def _():
        rn = tm // NORM_PIECES

        def norm_rows(k):
            r = slice(k * rn, (k + 1) * rn)
            h = h_ref[r]
            xn_s[r] = _rms(h, g_ref[...]).astype(BF16)
            o_ref[r] = h

        def up_rows(k):
            r = slice(k * rn, (k + 1) * rn)
            ugs[0][r] = jnp.dot(xn_s[r], wg_ref[...], preferred_element_type=F32)
            uvs[0][r] = jnp.dot(xn_s[r], wv_ref[...], preferred_element_type=F32)

        norm_rows(0)
        for k in range(NORM_PIECES):
            if k + 1 < NORM_PIECES:
                norm_rows(k + 1)
            up_rows(k)

    @pl.when(f == 1)
    def _():
        run(p_up=1, p_conv=0)

    for p in range(2):
        @pl.when((f >= 2) & (f < nf) & (f % 2 == p))
        def _():
            run(p_up=p, p_conv=1 - p, p_down=p)

    @pl.when(f == nf)
    def _():
        run(p_conv=(nf - 1) % 2, p_down=nf % 2)

    @pl.when(f == nf + 1)
    def _():
        run(p_down=(nf - 1) % 2)
        if final:
            o_ref[...] = _rms(o_ref[...], fg_ref[...])


def _ffn_layer(h, norm_g, w_in_bf16, conv_w, conv_b, w_out_bf16, final_g, *, final, tm=512, tf=256,
               tn=256):
    s_len, d = h.shape
    dff = w_out_bf16.shape[0]
    tn = min(tn, d)
    assert s_len % tm == 0 and dff % tf == 0 and d % tn == 0 and w_in_bf16.shape == (d, 2 * dff)
    nf = dff // tf
    assert nf >= 2
    conv_b2 = conv_b.reshape(1, 2 * dff)
    up = lambda f: jnp.minimum(f, nf - 1)
    cv = lambda f: jnp.clip(f - 1, 0, nf - 1)
    dn = lambda f: jnp.clip(f - 2, 0, nf - 1)
    return pl.pallas_call(
        functools.partial(_ffn_kernel, tm=tm, tn=tn, nf=nf, final=final),
        grid=(s_len // tm, nf + 2),
        in_specs=[
            pl.BlockSpec((tm, d), lambda i, f: (i, 0), pipeline_mode=pl.Buffered(1)),
            _const_spec((1, d)),
            pl.BlockSpec((d, tf), lambda i, f: (0, up(f))),
            pl.BlockSpec((d, tf), lambda i, f: (0, nf + up(f))),
            pl.BlockSpec((CONV_WIDTH, tf), lambda i, f: (0, cv(f))),
            pl.BlockSpec((CONV_WIDTH, tf), lambda i, f: (0, nf + cv(f))),
            pl.BlockSpec((1, tf), lambda i, f: (0, cv(f))),
            pl.BlockSpec((1, tf), lambda i, f: (0, nf + cv(f))),
            pl.BlockSpec((tf, d), lambda i, f: (dn(f), 0)),
            _const_spec((1, d)),
        ],
        out_specs=pl.BlockSpec((tm, d), lambda i, f: (i, 0)),
        out_shape=jax.ShapeDtypeStruct((s_len, d), F32),
        scratch_shapes=[
            pltpu.VMEM((tm, d), BF16),
            pltpu.VMEM((tm, tf), BF16),
            pltpu.VMEM((tm, tf), BF16),
            pltpu.VMEM((tm, tf), F32),
            pltpu.VMEM((tm, tf), F32),
            pltpu.VMEM((tm, tf), F32),
            pltpu.VMEM((tm, tf), F32),
            pltpu.VMEM((nf, CONV_HALO, tf), F32),
            pltpu.VMEM((nf, CONV_HALO, tf), F32),
        ],
        compiler_params=_params("arbitrary", "arbitrary"),
        name="conv_ffn",
    )(h, norm_g.reshape(1, d), w_in_bf16, w_in_bf16, conv_w, conv_w, conv_b2, conv_b2, w_out_bf16,
      final_g.reshape(1, d))


def _kv_kernel(h_ref, g_ref, wd_ref, lat_ref, cos_ref, sin_ref, wuk_ref, wuvt_ref,
               k_ref, vt_ref, c_s, ct_s, kr_s, *, lora, rope, nope, hg):
    gidx = pl.program_id(1)

    @pl.when(gidx == 0)
    def _():
        hn = _rms(h_ref[...], g_ref[...]).astype(BF16)
        ckv = jnp.dot(hn, wd_ref[...], preferred_element_type=F32)
        c = _rms(ckv[:, :lora], lat_ref[...])
        c_s[...] = c.astype(BF16)
        ct_s[...] = c.T.astype(BF16)
        kr = ckv[:, lora:lora + rope] * cos_ref[...] + ckv[:, lora + rope:] * sin_ref[...]
        kr_s[...] = kr.astype(BF16)

    kn = jnp.dot(c_s[...], wuk_ref[...], preferred_element_type=F32)
    vt = jnp.dot(wuvt_ref[...], ct_s[...], preferred_element_type=F32)
    vd = vt.shape[0] // hg
    tm = kn.shape[0]
    hd = nope + rope
    k_pad = k_ref.shape[2] - hd
    ones_rows = vt_ref.shape[2] - vd
    for hh in range(hg):
        k_ref[hh, :, 0:nope] = kn[:, hh * nope:(hh + 1) * nope].astype(BF16)
        k_ref[hh, :, nope:hd] = kr_s[...]
        if k_pad:
            k_ref[hh, :, hd:hd + k_pad] = jnp.zeros((tm, k_pad), BF16)
        vt_ref[hh, 0, 0:vd] = vt[hh * vd:(hh + 1) * vd].astype(BF16)
        vt_ref[hh, 0, vd:vd + ones_rows] = jnp.ones((ones_rows, tm), BF16)


def _kv_proj(h, norm_g, wd_ext_bf16, lat_g, cos2, sin2, wuk_bf16, wuvt_bf16, *, n_heads, nope, rope,
             hd_pad, ones_rows, tm=512, hg=16):
    s_len, d = h.shape
    lora = wuk_bf16.shape[0]
    vd = wuvt_bf16.shape[0] // n_heads
    hg = min(hg, n_heads)
    assert s_len % tm == 0 and n_heads % hg == 0 and wd_ext_bf16.shape == (d, lora + 2 * rope)
    return pl.pallas_call(
        functools.partial(_kv_kernel, lora=lora, rope=rope, nope=nope, hg=hg),
        grid=(s_len // tm, n_heads // hg),
        in_specs=[
            pl.BlockSpec((tm, d), lambda i, g: (i, 0)),
            _const_spec((1, d)),
            _const_spec((d, lora + 2 * rope)),
            _const_spec((1, lora)),
            pl.BlockSpec((tm, rope), lambda i, g: (i, 0)),
            pl.BlockSpec((tm, rope), lambda i, g: (i, 0)),
            pl.BlockSpec((lora, hg * nope), lambda i, g: (0, g)),
            pl.BlockSpec((hg * vd, lora), lambda i, g: (g, 0)),
        ],
        out_specs=[
            pl.BlockSpec((hg, tm, hd_pad), lambda i, g: (g, i, 0)),
            pl.BlockSpec((hg, 1, vd + ones_rows, tm), lambda i, g: (g, i, 0, 0)),
        ],
        out_shape=[
            jax.ShapeDtypeStruct((n_heads, s_len, hd_pad), BF16),
            jax.ShapeDtypeStruct((n_heads, s_len // tm, vd + ones_rows, tm), BF16),
        ],
        scratch_shapes=[
            pltpu.VMEM((tm, lora), BF16),
            pltpu.VMEM((lora, tm), BF16),
            pltpu.VMEM((tm, rope), BF16),
        ],
        compiler_params=_params("parallel", "arbitrary"),
        name="kv_proj",
    )(h, norm_g.reshape(1, d), wd_ext_bf16, lat_g.reshape(1, lora), cos2, sin2, wuk_bf16, wuvt_bf16)


def _q_kernel(h_ref, g_ref, wdt_ref, lat_ref, cos_ref, sin_ref, wut_ref, qt_ref, cqt_s,
              *, nope, rope, hg, q_scale):
    gidx = pl.program_id(1)

    @pl.when(gidx == 0)
    def _():
        xn = _rms(h_ref[...], g_ref[...]).astype(BF16)
        cqt = lax.dot_general(wdt_ref[...], xn, (((1,), (1,)), ((), ())),
                              preferred_element_type=F32)
        ms = jnp.mean(cqt * cqt, axis=0, keepdims=True)
        cqt_s[...] = (cqt * lax.rsqrt(ms + EPS) * lat_ref[...]).astype(BF16)

    qt = jnp.dot(wut_ref[...], cqt_s[...], preferred_element_type=F32) * q_scale
    cos_t = cos_ref[...]
    sin_t = sin_ref[...]
    hd = nope + rope
    half = rope // 2
    pad = qt_ref.shape[2] - hd
    for hh in range(hg):
        b = hh * hd
        qt_ref[hh, 0, 0:nope] = qt[b:b + nope].astype(BF16)
        x1 = qt[b + nope:b + nope + half]
        x2 = qt[b + nope + half:b + hd]
        qt_ref[hh, 0, nope:nope + half] = (x1 * cos_t - x2 * sin_t).astype(BF16)
        qt_ref[hh, 0, nope + half:hd] = (x2 * cos_t + x1 * sin_t).astype(BF16)
        if pad:
            qt_ref[hh, 0, hd:hd + pad] = jnp.zeros((pad, qt.shape[1]), BF16)


def _q_proj(h, norm_g, wdqt_bf16, lat_g, cos_t, sin_t, wuqt_bf16, *, n_heads, nope, rope, hd_pad,
            q_scale, tm=512, hg=16):
    s_len, d = h.shape
    qlora = wdqt_bf16.shape[0]
    hd = nope + rope
    hg = min(hg, n_heads)
    assert s_len % tm == 0 and n_heads % hg == 0 and wuqt_bf16.shape == (n_heads * hd, qlora)
    return pl.pallas_call(
        functools.partial(_q_kernel, nope=nope, rope=rope, hg=hg, q_scale=q_scale),
        grid=(s_len // tm, n_heads // hg),
        in_specs=[
            pl.BlockSpec((tm, d), lambda i, g: (i, 0)),
            _const_spec((1, d)),
            _const_spec((qlora, d)),
            _const_spec((qlora, 1)),
            pl.BlockSpec((rope // 2, tm), lambda i, g: (0, i)),
            pl.BlockSpec((rope // 2, tm), lambda i, g: (0, i)),
            pl.BlockSpec((hg * hd, qlora), lambda i, g: (g, 0)),
        ],
        out_specs=pl.BlockSpec((hg, 1, hd_pad, tm), lambda i, g: (g, i, 0, 0)),
        out_shape=jax.ShapeDtypeStruct((n_heads, s_len // tm, hd_pad, tm), BF16),
        scratch_shapes=[pltpu.VMEM((qlora, tm), BF16)],
        compiler_params=_params("parallel", "arbitrary"),
        name="q_proj",
    )(h, norm_g.reshape(1, d), wdqt_bf16, lat_g.reshape(qlora, 1), cos_t, sin_t, wuqt_bf16)


def _attn_kernel(qt_ref, k_ref, vt_ref, *rest, tq, tk, cb, vd, n_cast):
    w_f32 = rest[:n_cast]
    o_ref = rest[n_cast]
    w_bf16 = rest[n_cast + 1:2 * n_cast + 1]
    s0_ref, s1_ref, cmax0_ref, cmax1_ref, acc_ref, m_ref = rest[2 * n_cast + 1:]
    cmax_buf = (cmax0_ref, cmax1_ref)
    for src, dst in zip(w_f32, w_bf16):
        dst[...] = src[0].astype(BF16)

    i = pl.program_id(1)
    tile = qt_ref.shape[3]
    r = tq // tk
    s_buf = (s0_ref, s1_ref)

    def scores(j, slot, c0, c1):
        kblk = k_ref[0, pl.ds(pl.multiple_of(j * tk, tk), tk), :]
        qt = qt_ref[0, c0 // tile, :, c0 % tile:c0 % tile + (c1 - c0)]
        s = jnp.dot(kblk, qt, preferred_element_type=F32)
        s_buf[slot][:, c0:c1] = s
        cmax_buf[slot][:, c0:c1] = jnp.max(s, axis=0, keepdims=True)

    def weights_values(j, slot, c0, c1, key_offset):
        s = s_buf[slot][:, c0:c1]
        if key_offset is not None and key_offset + tk - 1 > c0:
            kpos = lax.broadcasted_iota(jnp.int32, s.shape, 0) + key_offset
            qpos = lax.broadcasted_iota(jnp.int32, s.shape, 1) + c0
            s = jnp.where(kpos <= qpos, s, MASK_VALUE)
            cmax = jnp.max(s, axis=0, keepdims=True)
        else:
            cmax = cmax_buf[slot][:, c0:c1]
        m = m_ref[:, c0:c1]
        m_new = jnp.maximum(m, cmax)
        alpha = jnp.exp2(m - m_new)
        p = jnp.exp2(s - m_new)
        m_ref[:, c0:c1] = m_new
        vt = vt_ref[0, j]
        acc_ref[:, c0:c1] = alpha * acc_ref[:, c0:c1] + jnp.dot(vt, p.astype(BF16),
                                                               preferred_element_type=F32)

    def col_blocks(lo):
        return [(c0, min(c0 + cb, tq)) for c0 in range(lo, tq, cb)]

    def stage(j, slot, lo=0, diagonal=False, next_lo=0):
        cur = col_blocks(lo)
        nxt = col_blocks(next_lo) if next_lo is not None else []
        for idx in range(max(len(cur), len(nxt))):
            if idx < len(nxt):
                scores(j + 1, 1 - slot, *nxt[idx])
            if idx < len(cur):
                weights_values(j, slot, *cur[idx], key_offset=lo if diagonal else None)

    m_ref[...] = jnp.full(m_ref.shape, MASK_VALUE, F32)
    acc_ref[...] = jnp.zeros(acc_ref.shape, F32)
    for c0, c1 in col_blocks(0):
        scores(0, 0, c0, c1)

    def group(g, carry):
        for c in range(r):
            stage(g * r + c, c % 2)
        return carry

    lax.fori_loop(0, i, group, 0)
    for c in range(r):
        stage(i * r + c, c % 2, lo=c * tk, diagonal=True, next_lo=(c + 1) * tk if c < r - 1 else None)
    o_ref[...] = (acc_ref[0:vd] / acc_ref[vd:vd + 1]).T.astype(o_ref.dtype)


def _cast_block_rows(n_rows, n_steps):
    for rb in range(BF16_SUBLANES, n_rows + 1, BF16_SUBLANES):
        if n_rows % rb == 0 and n_rows // rb <= n_steps:
            return rb
    return None


def _attention(qt, k, vt, stacked_weights, *, vd, tq=2048, tk=512, cb=512):
    n_heads, n_tiles, hd, tile = qt.shape
    s_len = n_tiles * tile
    vda = vt.shape[2]
    tq = min(tq, s_len)
    cb = min(cb, tq)
    assert k.shape == (n_heads, s_len, hd) and vt.shape == (n_heads, n_tiles, vda, tile) and vda > vd
    assert s_len % tq == 0 and tq % (2 * tk) == 0 and tk == tile and tile % cb == 0
    nq = s_len // tq
    cast_in, cast_out, cast_shapes = [], [], []
    for w, layer in stacked_weights:
        _, n_rows, n_cols = w.shape
        rb = _cast_block_rows(n_rows, n_heads * nq)
        assert rb is not None
        blk = lambda h, i, last=n_rows // rb - 1: jnp.minimum(h * nq + i, last)
        cast_in.append(pl.BlockSpec((1, rb, n_cols), lambda h, i, blk=blk, layer=layer: (layer, blk(h, i), 0)))
        cast_out.append(pl.BlockSpec((rb, n_cols), lambda h, i, blk=blk: (blk(h, i), 0)))
        cast_shapes.append(jax.ShapeDtypeStruct((n_rows, n_cols), BF16))
    outs = pl.pallas_call(
        functools.partial(_attn_kernel, tq=tq, tk=tk, cb=cb, vd=vd, n_cast=len(stacked_weights)),
        grid=(n_heads, nq),
        in_specs=[
            pl.BlockSpec((1, tq // tile, hd, tile), lambda h, i: (h, i, 0, 0)),
            pl.BlockSpec((1, s_len, hd), lambda h, i: (h, 0, 0)),
            pl.BlockSpec((1, n_tiles, vda, tile), lambda h, i: (h, 0, 0, 0)),
        ] + cast_in,
        out_specs=[pl.BlockSpec((tq, vd), lambda h, i: (i, h))] + cast_out,
        out_shape=[jax.ShapeDtypeStruct((s_len, n_heads * vd), BF16)] + cast_shapes,
        scratch_shapes=[
            pltpu.VMEM((tk, tq), F32),
            pltpu.VMEM((tk, tq), F32),
            pltpu.VMEM((1, tq), F32),
            pltpu.VMEM((1, tq), F32),
            pltpu.VMEM((vda, tq), F32),
            pltpu.VMEM((1, tq), F32),
        ],
        compiler_params=_params("arbitrary", "arbitrary"),
        name="mla_attention",
    )(qt, k, vt, *[w for w, _ in stacked_weights])
    return outs[0], outs[1:]


def _oproj_kernel(a_ref, w_ref, r_ref, o_ref):
    o_ref[...] = r_ref[...] + jnp.dot(a_ref[...], w_ref[...], preferred_element_type=F32)


def _out_proj(a_bf16, w_bf16, resid, *, tm=512, tn=1024):
    s_len, kdim = a_bf16.shape
    d = w_bf16.shape[1]
    tn = min(tn, d)
    assert s_len % tm == 0 and d % tn == 0
    return pl.pallas_call(
        _oproj_kernel,
        grid=(s_len // tm, d // tn),
        in_specs=[
            pl.BlockSpec((tm, kdim), lambda i, j: (i, 0)),
            pl.BlockSpec((kdim, tn), lambda i, j: (0, j)),
            pl.BlockSpec((tm, tn), lambda i, j: (i, j)),
        ],
        out_specs=pl.BlockSpec((tm, tn), lambda i, j: (i, j)),
        out_shape=jax.ShapeDtypeStruct((s_len, d), F32),
        compiler_params=_params("parallel", "parallel"),
        name="attn_out_proj",
    )(a_bf16, w_bf16, resid)


def _trunk(x, a_norm, a_pool_w, a_scale, kv_norm, w_dkv, kv_lat_norm, w_ukv, b_norm, w_dq, q_lat_norm,
           w_uq, w_o, ffn_norm, ffn_w_in, ffn_conv_w, ffn_conv_b, ffn_w_out, final_norm,
           *, pool_ts=256, ffn_tm=512, ffn_tf=256, proj_tm=512, attn_tq=2048, attn_tk=512, attn_cb=256,
           oproj_tm=512):
    batch, s_len, d = x.shape
    depth = ffn_norm.shape[0]
    n_a = a_norm.shape[0]
    n_b = b_norm.shape[0]
    assert n_a + n_b == depth and n_a >= 1
    lora, n_heads, kvd = w_ukv.shape
    hd = w_uq.shape[-1]
    rope = w_dkv.shape[1] - lora
    nope = hd - rope
    vd = kvd - nope
    half = rope // 2
    hd_pad = -(-hd // LANES) * LANES

    inv_freq = ROPE_THETA ** (-jnp.arange(0, rope, 2, dtype=F32) / rope)
    ang = jnp.arange(s_len, dtype=F32)[:, None] * inv_freq[None, :]
    cos, sin = jnp.cos(ang), jnp.sin(ang)
    cos2 = jnp.concatenate([cos, cos], axis=1)
    sin2 = jnp.concatenate([-sin, sin], axis=1)
    cos_t, sin_t = cos.T, sin.T

    bf = lambda w: w.astype(BF16)
    wd_ext = bf(jnp.concatenate([w_dkv, w_dkv[:, lora + half:], w_dkv[:, lora:lora + half]], axis=1))
    wuk = bf(w_ukv[:, :, :nope].reshape(lora, n_heads * nope))
    wuvt = bf(w_ukv[:, :, nope:].reshape(lora, n_heads * vd).T)
    final_g = final_norm

    outs = []
    ffn_bf16 = {}
    wo_bf16 = {}
    for b in range(batch):
        h = x.reshape(s_len, d) if batch == 1 else x[b]
        shared = None
        for layer in range(depth):
            last = layer == depth - 1
            if layer < n_a:
                h = _pool_layer(h, a_norm[layer], bf(a_pool_w[layer]), a_scale[layer], ts=pool_ts)
            else:
                j = layer - n_a
                qlora = w_dq.shape[2]
                qt = _q_proj(h, b_norm[j], bf(w_dq[j].T), q_lat_norm[j], cos_t, sin_t,
                             bf(w_uq[j].reshape(qlora, n_heads * hd).T),
                             n_heads=n_heads, nope=nope, rope=rope, hd_pad=hd_pad,
                             q_scale=hd ** -0.5 * math.log2(math.e), tm=proj_tm)
                first = layer not in ffn_bf16
                to_cast = ((ffn_w_in, layer), (ffn_w_out, layer), (w_o, j)) if first else ()
                o, cast = _attention(qt, shared[0], shared[1], to_cast, vd=vd, tq=attn_tq, tk=attn_tk,
                                     cb=attn_cb)
                if first:
                    ffn_bf16[layer] = cast[:2]
                    wo_bf16[j] = cast[2]
                h = _out_proj(o, wo_bf16[j], h, tm=oproj_tm)
            if layer not in ffn_bf16:
                ffn_bf16[layer] = (bf(ffn_w_in[layer]), bf(ffn_w_out[layer]))
            h = _ffn_layer(h, ffn_norm[layer], ffn_bf16[layer][0], ffn_conv_w[layer], ffn_conv_b[layer],
                           ffn_bf16[layer][1], final_g, final=last, tm=ffn_tm, tf=ffn_tf)
            if layer == n_a - 1:
                shared = _kv_proj(h, kv_norm, wd_ext, kv_lat_norm, cos2, sin2, wuk, wuvt,
                                  n_heads=n_heads, nope=nope, rope=rope, hd_pad=hd_pad,
                                  ones_rows=BF16_SUBLANES, tm=proj_tm)
        outs.append(h)
    return outs[0].reshape(1, s_len, d) if batch == 1 else jnp.stack(outs, axis=0)


def kernel(x, a_norm, a_pool_w, a_scale, kv_norm, w_dkv, kv_lat_norm, w_ukv, b_norm, w_dq, q_lat_norm,
           w_uq, w_o, ffn_norm, ffn_w_in, ffn_conv_w, ffn_conv_b, ffn_w_out, final_norm):
    return _trunk(x, a_norm, a_pool_w, a_scale, kv_norm, w_dkv, kv_lat_norm, w_ukv, b_norm, w_dq,
                  q_lat_norm, w_uq, w_o, ffn_norm, ffn_w_in, ffn_conv_w, ffn_conv_b, ffn_w_out,
                  final_norm)
```

```python
import functools
import math

import jax
import jax.numpy as jnp
from jax import lax
from jax.experimental import pallas as pl
from jax.experimental.pallas import tpu as pltpu

EPS = 1e-6
ROPE_THETA = 10000.0
POOL_WINDOWS = (2, 4, 8, 16)
CONV_WIDTH = 3
POOL_HALO = 16
CONV_HALO = 8
NORM_PIECES = 4
CONV_PIECES = 16
MASK_VALUE = -1e30
VMEM_LIMIT_BYTES = 56 * 1024 * 1024
LANES = 128
BF16_SUBLANES = 16

F32 = jnp.float32
BF16 = jnp.bfloat16


def _rms(x, g):
    return x * lax.rsqrt(jnp.mean(x * x, axis=-1, keepdims=True) + EPS) * g


def _params(*sem):
    return pltpu.CompilerParams(dimension_semantics=sem, vmem_limit_bytes=VMEM_LIMIT_BYTES)


def _const_spec(shape):
    nd = len(shape)
    return pl.BlockSpec(shape, lambda *_: (0,) * nd, pipeline_mode=pl.Buffered(1))


def _pool_kernel(x_ref, halo_ref, g_ref, w_ref, sc_ref, o_ref, *, ts, gc):
    i = pl.program_id(0)
    g = g_ref[...]
    x = x_ref[...]
    xn = _rms(x, g)
    hn = jnp.where(i == 0, 0.0, _rms(halo_ref[...], g))
    pos = i * ts + lax.broadcasted_iota(jnp.int32, (ts, 1), 0)
    for gi, w in enumerate(POOL_WINDOWS):
        sl = slice(gi * gc, (gi + 1) * gc)
        xg = xn[:, sl]
        s = jnp.concatenate([hn[:, sl], xg], axis=0)
        k = 1
        while k < w:
            s = s + pltpu.roll(s, k, axis=0)
            k *= 2
        cnt = jnp.minimum(pos + 1, w).astype(F32)
        pooled = s[POOL_HALO:] / cnt - xg
        mixed = jnp.dot(pooled.astype(BF16), w_ref[gi], preferred_element_type=F32)
        o_ref[:, sl] = x[:, sl] + mixed * sc_ref[:, sl]


def _pool_layer(x, norm_g, w_bf16, scale, *, ts=256):
    s_len, d = x.shape
    ng, gc, _ = w_bf16.shape
    assert ng == len(POOL_WINDOWS) and ng * gc == d and s_len % ts == 0 and ts % POOL_HALO == 0
    per = ts // POOL_HALO
    return pl.pallas_call(
        functools.partial(_pool_kernel, ts=ts, gc=gc),
        grid=(s_len // ts,),
        in_specs=[
            pl.BlockSpec((ts, d), lambda i: (i, 0)),
            pl.BlockSpec((POOL_HALO, d), lambda i: (jnp.maximum(i * per - 1, 0), 0)),
            _const_spec((1, d)),
            _const_spec((ng, gc, gc)),
            _const_spec((1, d)),
        ],
        out_specs=pl.BlockSpec((ts, d), lambda i: (i, 0)),
        out_shape=jax.ShapeDtypeStruct((s_len, d), F32),
        compiler_params=_params("parallel"),
        name="pool_layer",
    )(x, x, norm_g.reshape(1, d), w_bf16, scale.reshape(1, d))


def _ffn_kernel(h_ref, g_ref, wg_ref, wv_ref, cwg_ref, cwv_ref, bg_ref, bv_ref, wo_ref, fg_ref,
                o_ref, xn_s, a0_s, a1_s, ug0_s, ug1_s, uv0_s, uv1_s, cg_s, cv_s, *, tm, tn, nf, final):
    i = pl.program_id(0)
    f = pl.program_id(1)
    acts = (a0_s, a1_s)
    ugs = (ug0_s, ug1_s)
    uvs = (uv0_s, uv1_s)

    c = jnp.clip(f - 1, 0, nf - 1)

    @pl.when((i == 0) & (f >= 1) & (f <= nf))
    def _():
        cg_s[c] = jnp.zeros(cg_s.shape[1:], F32)
        cv_s[c] = jnp.zeros(cv_s.shape[1:], F32)

    n_down = o_ref.shape[1] // tn
    rows = tm // CONV_PIECES

    def up_piece(p, which, half):
        u_s, w_ref = ((ugs[p], wg_ref), (uvs[p], wv_ref))[which]
        r = slice(half * (tm // 2), (half + 1) * (tm // 2))
        u_s[r] = jnp.dot(xn_s[r], w_ref[...], preferred_element_type=F32)

    def conv_rows(u_s, cw_ref, b_ref, c_s, r0):
        prev = c_s[c] if r0 == 0 else u_s[r0 - CONV_HALO:r0]
        u = u_s[r0:r0 + rows]
        cat = jnp.concatenate([prev, u], axis=0)
        u1 = pltpu.roll(cat, 1, axis=0)[CONV_HALO:]
        u2 = pltpu.roll(cat, 2, axis=0)[CONV_HALO:]
        cw = cw_ref[...]
        return cw[0:1] * u2 + cw[1:2] * u1 + cw[2:3] * u + b_ref[...]

    def conv_piece(p, k):
        r0 = k * rows
        gate = conv_rows(ugs[p], cwg_ref, bg_ref, cg_s, r0)
        val = conv_rows(uvs[p], cwv_ref, bv_ref, cv_s, r0)
        acts[p][r0:r0 + rows] = (gate * (1.0 / (1.0 + jnp.exp(-gate))) * val).astype(BF16)
        if k == CONV_PIECES - 1:
            cg_s[c] = ugs[p][tm - CONV_HALO:]
            cv_s[c] = uvs[p][tm - CONV_HALO:]

    def down_piece(p, n):
        sl = slice(n * tn, (n + 1) * tn)
        o_ref[:, sl] += jnp.dot(acts[p][...], wo_ref[:, sl], preferred_element_type=F32)

    def run(p_up=None, p_conv=None, p_down=None):
        mxu, vpu = [], []
        if p_up is not None:
            mxu += [functools.partial(up_piece, p_up, w, half) for half in range(2) for w in range(2)]
        if p_down is not None:
            mxu += [functools.partial(down_piece, p_down, n) for n in range(n_down)]
        if p_conv is not None:
            vpu += [functools.partial(conv_piece, p_conv, k) for k in range(CONV_PIECES)]
        for idx in range(max(len(mxu), len(vpu))):
            if idx < len(mxu):
                mxu[idx]()
            if idx < len(vpu):
                vpu[idx]()

    @pl.when(f == 0)
    def _():
        rn = tm // NORM_PIECES

        def norm_rows(k):
            r = slice(k * rn, (k + 1) * rn)
            h = h_ref[r]
            xn_s[r] = _rms(h, g_ref[...]).astype(BF16)
            o_ref[r] = h

        def up_rows(k):
            r = slice(k * rn, (k + 1) * rn)
            ugs[0][r] = jnp.dot(xn_s[r], wg_ref[...], preferred_element_type=F32)
            uvs[0][r] = jnp.dot(xn_s[r], wv_ref[...], preferred_element_type=F32)

        norm_rows(0)
        for k in range(NORM_PIECES):
            if k + 1 < NORM_PIECES:
                norm_rows(k + 1)
            up_rows(k)

    @pl.when(f == 1)
    def _():
        run(p_up=1, p_conv=0)

    for p in range(2):
        @pl.when((f >= 2) & (f < nf) & (f % 2 == p))
        def _():
            run(p_up=p, p_conv=1 - p, p_down=p)

    @pl.when(f == nf)
    def _():
        run(p_conv=(nf - 1) % 2, p_down=nf % 2)

    @pl.when(f == nf + 1)
    def _():
        run(p_down=(nf - 1) % 2)
        if final:
            o_ref[...] = _rms(o_ref[...], fg_ref[...])


def _ffn_layer(h, norm_g, w_in_bf16, conv_w, conv_b, w_out_bf16, final_g, *, final, tm=512, tf=256,
               tn=256):
    s_len, d = h.shape
    dff = w_out_bf16.shape[0]
    tn = min(tn, d)
    assert s_len % tm == 0 and dff % tf == 0 and d % tn == 0 and w_in_bf16.shape == (d, 2 * dff)
    nf = dff // tf
    assert nf >= 2
    conv_b2 = conv_b.reshape(1, 2 * dff)
    up = lambda f: jnp.minimum(f, nf - 1)
    cv = lambda f: jnp.clip(f - 1, 0, nf - 1)
    dn = lambda f: jnp.clip(f - 2, 0, nf - 1)
    return pl.pallas_call(
        functools.partial(_ffn_kernel, tm=tm, tn=tn, nf=nf, final=final),
        grid=(s_len // tm, nf + 2),
        in_specs=[
            pl.BlockSpec((tm, d), lambda i, f: (i, 0), pipeline_mode=pl.Buffered(1)),
            _const_spec((1, d)),
            pl.BlockSpec((d, tf), lambda i, f: (0, up(f))),
            pl.BlockSpec((d, tf), lambda i, f: (0, nf + up(f))),
            pl.BlockSpec((CONV_WIDTH, tf), lambda i, f: (0, cv(f))),
            pl.BlockSpec((CONV_WIDTH, tf), lambda i, f: (0, nf + cv(f))),
            pl.BlockSpec((1, tf), lambda i, f: (0, cv(f))),
            pl.BlockSpec((1, tf), lambda i, f: (0, nf + cv(f))),
            pl.BlockSpec((tf, d), lambda i, f: (dn(f), 0)),
            _const_spec((1, d)),
        ],
        out_specs=pl.BlockSpec((tm, d), lambda i, f: (i, 0)),
        out_shape=jax.ShapeDtypeStruct((s_len, d), F32),
        scratch_shapes=[
            pltpu.VMEM((tm, d), BF16),
            pltpu.VMEM((tm, tf), BF16),
            pltpu.VMEM((tm, tf), BF16),
            pltpu.VMEM((tm, tf), F32),
            pltpu.VMEM((tm, tf), F32),
            pltpu.VMEM((tm, tf), F32),
            pltpu.VMEM((tm, tf), F32),
            pltpu.VMEM((nf, CONV_HALO, tf), F32),
            pltpu.VMEM((nf, CONV_HALO, tf), F32),
        ],
        compiler_params=_params("arbitrary", "arbitrary"),
        name="conv_ffn",
    )(h, norm_g.reshape(1, d), w_in_bf16, w_in_bf16, conv_w, conv_w, conv_b2, conv_b2, w_out_bf16,
      final_g.reshape(1, d))


def _kv_kernel(h_ref, g_ref, wd_ref, lat_ref, cos_ref, sin_ref, wuk_ref, wuvt_ref,
               k_ref, vt_ref, c_s, ct_s, kr_s, *, lora, rope, nope, vd, hg):
    gidx = pl.program_id(1)

    @pl.when(gidx == 0)
    def _():
        hn = _rms(h_ref[...], g_ref[...]).astype(BF16)
        ckv = jnp.dot(hn, wd_ref[...], preferred_element_type=F32)
        c = _rms(ckv[:, :lora], lat_ref[...])
        c_s[...] = c.astype(BF16)
        ct_s[...] = c.T.astype(BF16)
        kr = ckv[:, lora:lora + rope] * cos_ref[...] + ckv[:, lora + rope:] * sin_ref[...]
        kr_s[...] = kr.astype(BF16)

    wuk = wuk_ref[:, pl.ds(pl.multiple_of(gidx * (hg * nope), hg * nope), hg * nope)]
    wuvt = wuvt_ref[pl.ds(pl.multiple_of(gidx * (hg * vd), hg * vd), hg * vd), :]
    kn = jnp.dot(c_s[...], wuk, preferred_element_type=F32)
    vt = jnp.dot(wuvt, ct_s[...], preferred_element_type=F32)
    tm = kn.shape[0]
    hd = nope + rope
    k_pad = k_ref.shape[2] - hd
    ones_rows = vt_ref.shape[2] - vd
    for hh in range(hg):
        k_ref[hh, :, 0:nope] = kn[:, hh * nope:(hh + 1) * nope].astype(BF16)
        k_ref[hh, :, nope:hd] = kr_s[...]
        if k_pad:
            k_ref[hh, :, hd:hd + k_pad] = jnp.zeros((tm, k_pad), BF16)
        vt_ref[hh, 0, 0:vd] = vt[hh * vd:(hh + 1) * vd].astype(BF16)
        vt_ref[hh, 0, vd:vd + ones_rows] = jnp.ones((ones_rows, tm), BF16)


def _kv_proj(h, norm_g, wd_ext_bf16, lat_g, cos2, sin2, wuk_bf16, wuvt_bf16, *, n_heads, nope, rope,
             hd_pad, ones_rows, tm=512, hg=16):
    s_len, d = h.shape
    lora = wuk_bf16.shape[0]
    vd = wuvt_bf16.shape[0] // n_heads
    hg = min(hg, n_heads)
    assert s_len % tm == 0 and n_heads % hg == 0 and wd_ext_bf16.shape == (d, lora + 2 * rope)
    return pl.pallas_call(
        functools.partial(_kv_kernel, lora=lora, rope=rope, nope=nope, vd=vd, hg=hg),
        grid=(s_len // tm, n_heads // hg),
        in_specs=[
            pl.BlockSpec((tm, d), lambda i, g: (i, 0)),
            _const_spec((1, d)),
            _const_spec((d, lora + 2 * rope)),
            _const_spec((1, lora)),
            pl.BlockSpec((tm, rope), lambda i, g: (i, 0)),
            pl.BlockSpec((tm, rope), lambda i, g: (i, 0)),
            _const_spec((lora, n_heads * nope)),
            _const_spec((n_heads * vd, lora)),
        ],
        out_specs=[
            pl.BlockSpec((hg, tm, hd_pad), lambda i, g: (g, i, 0)),
            pl.BlockSpec((hg, 1, vd + ones_rows, tm), lambda i, g: (g, i, 0, 0)),
        ],
        out_shape=[
            jax.ShapeDtypeStruct((n_heads, s_len, hd_pad), BF16),
            jax.ShapeDtypeStruct((n_heads, s_len // tm, vd + ones_rows, tm), BF16),
        ],
        scratch_shapes=[
            pltpu.VMEM((tm, lora), BF16),
            pltpu.VMEM((lora, tm), BF16),
            pltpu.VMEM((tm, rope), BF16),
        ],
        compiler_params=_params("parallel", "arbitrary"),
        name="kv_proj",
    )(h, norm_g.reshape(1, d), wd_ext_bf16, lat_g.reshape(1, lora), cos2, sin2, wuk_bf16, wuvt_bf16)


def _q_kernel(h_ref, g_ref, wdt_ref, lat_ref, cos_ref, sin_ref, wut_ref, qt_ref, cqt_s,
              *, nope, rope, hg, q_scale):
    gidx = pl.program_id(1)

    @pl.when(gidx == 0)
    def _():
        xn = _rms(h_ref[...], g_ref[...]).astype(BF16)
        cqt = lax.dot_general(wdt_ref[...], xn, (((1,), (1,)), ((), ())),
                              preferred_element_type=F32)
        ms = jnp.mean(cqt * cqt, axis=0, keepdims=True)
        cqt_s[...] = (cqt * lax.rsqrt(ms + EPS) * lat_ref[...]).astype(BF16)

    rows = hg * (nope + rope)
    wut = wut_ref[pl.ds(pl.multiple_of(gidx * rows, rows), rows), :]
    qt = jnp.dot(wut, cqt_s[...], preferred_element_type=F32) * q_scale
    cos_t = cos_ref[...]
    sin_t = sin_ref[...]
    hd = nope + rope
    half = rope // 2
    pad = qt_ref.shape[2] - hd
    for hh in range(hg):
        b = hh * hd
        qt_ref[hh, 0, 0:nope] = qt[b:b + nope].astype(BF16)
        x1 = qt[b + nope:b + nope + half]
        x2 = qt[b + nope + half:b + hd]
        qt_ref[hh, 0, nope:nope + half] = (x1 * cos_t - x2 * sin_t).astype(BF16)
        qt_ref[hh, 0, nope + half:hd] = (x2 * cos_t + x1 * sin_t).astype(BF16)
        if pad:
            qt_ref[hh, 0, hd:hd + pad] = jnp.zeros((pad, qt.shape[1]), BF16)


def _q_proj(h, norm_g, wdqt_bf16, lat_g, cos_t, sin_t, wuqt_bf16, *, n_heads, nope, rope, hd_pad,
            q_scale, tm=512, hg=16):
    s_len, d = h.shape
    qlora = wdqt_bf16.shape[0]
    hd = nope + rope
    hg = min(hg, n_heads)
    assert s_len % tm == 0 and n_heads % hg == 0 and wuqt_bf16.shape == (n_heads * hd, qlora)
    return pl.pallas_call(
        functools.partial(_q_kernel, nope=nope, rope=rope, hg=hg, q_scale=q_scale),
        grid=(s_len // tm, n_heads // hg),
        in_specs=[
            pl.BlockSpec((tm, d), lambda i, g: (i, 0)),
            _const_spec((1, d)),
            _const_spec((qlora, d)),
            _const_spec((qlora, 1)),
            pl.BlockSpec((rope // 2, tm), lambda i, g: (0, i)),
            pl.BlockSpec((rope // 2, tm), lambda i, g: (0, i)),
            _const_spec((n_heads * hd, qlora)),
        ],
        out_specs=pl.BlockSpec((hg, 1, hd_pad, tm), lambda i, g: (g, i, 0, 0)),
        out_shape=jax.ShapeDtypeStruct((n_heads, s_len // tm, hd_pad, tm), BF16),
        scratch_shapes=[pltpu.VMEM((qlora, tm), BF16)],
        compiler_params=_params("parallel", "arbitrary"),
        name="q_proj",
    )(h, norm_g.reshape(1, d), wdqt_bf16, lat_g.reshape(qlora, 1), cos_t, sin_t, wuqt_bf16)


def _attn_kernel(qt_ref, k_ref, vt_ref, *rest, tq, tk, cb, vd, n_cast):
    w_f32 = rest[:n_cast]
    o_ref = rest[n_cast]
    w_bf16 = rest[n_cast + 1:2 * n_cast + 1]
    s0_ref, s1_ref, cmax0_ref, cmax1_ref, acc_ref, m_ref = rest[2 * n_cast + 1:]
    cmax_buf = (cmax0_ref, cmax1_ref)
    for src, dst in zip(w_f32, w_bf16):
        dst[...] = src[0].astype(BF16)

    i = pl.program_id(1)
    tile = qt_ref.shape[3]
    r = tq // tk
    s_buf = (s0_ref, s1_ref)

    def scores(j, slot, c0, c1):
        kblk = k_ref[0, pl.ds(pl.multiple_of(j * tk, tk), tk), :]
        qt = qt_ref[0, c0 // tile, :, c0 % tile:c0 % tile + (c1 - c0)]
        s = jnp.dot(kblk, qt, preferred_element_type=F32)
        s_buf[slot][:, c0:c1] = s
        cmax_buf[slot][:, c0:c1] = jnp.max(s, axis=0, keepdims=True)

    def weights_values(j, slot, c0, c1, key_offset):
        s = s_buf[slot][:, c0:c1]
        if key_offset is not None and key_offset + tk - 1 > c0:
            kpos = lax.broadcasted_iota(jnp.int32, s.shape, 0) + key_offset
            qpos = lax.broadcasted_iota(jnp.int32, s.shape, 1) + c0
            s = jnp.where(kpos <= qpos, s, MASK_VALUE)
            cmax = jnp.max(s, axis=0, keepdims=True)
        else:
            cmax = cmax_buf[slot][:, c0:c1]
        m = m_ref[:, c0:c1]
        m_new = jnp.maximum(m, cmax)
        alpha = jnp.exp2(m - m_new)
        p = jnp.exp2(s - m_new)
        m_ref[:, c0:c1] = m_new
        vt = vt_ref[0, j]
        acc_ref[:, c0:c1] = alpha * acc_ref[:, c0:c1] + jnp.dot(vt, p.astype(BF16),
                                                               preferred_element_type=F32)

    def col_blocks(lo):
        return [(c0, min(c0 + cb, tq)) for c0 in range(lo, tq, cb)]

    def stage(j, slot, lo=0, diagonal=False, next_lo=0):
        cur = col_blocks(lo)
        nxt = col_blocks(next_lo) if next_lo is not None else []
        for idx in range(max(len(cur), len(nxt))):
            if idx < len(nxt):
                scores(j + 1, 1 - slot, *nxt[idx])
            if idx < len(cur):
                weights_values(j, slot, *cur[idx], key_offset=lo if diagonal else None)

    m_ref[...] = jnp.full(m_ref.shape, MASK_VALUE, F32)
    acc_ref[...] = jnp.zeros(acc_ref.shape, F32)
    for c0, c1 in col_blocks(0):
        scores(0, 0, c0, c1)

    def group(g, carry):
        for c in range(r):
            stage(g * r + c, c % 2)
        return carry

    lax.fori_loop(0, i, group, 0)
    for c in range(r):
        stage(i * r + c, c % 2, lo=c * tk, diagonal=True, next_lo=(c + 1) * tk if c < r - 1 else None)
    o_ref[...] = (acc_ref[0:vd] / acc_ref[vd:vd + 1]).T.astype(o_ref.dtype)


def _cast_block_rows(n_rows, n_steps):
    for rb in range(BF16_SUBLANES, n_rows + 1, BF16_SUBLANES):
        if n_rows % rb == 0 and n_rows // rb <= n_steps:
            return rb
    return None


def _attention(qt, k, vt, stacked_weights, *, vd, tq=2048, tk=512, cb=512):
    n_heads, n_tiles, hd, tile = qt.shape
    s_len = n_tiles * tile
    vda = vt.shape[2]
    tq = min(tq, s_len)
    cb = min(cb, tq)
    assert k.shape == (n_heads, s_len, hd) and vt.shape == (n_heads, n_tiles, vda, tile) and vda > vd
    assert s_len % tq == 0 and tq % (2 * tk) == 0 and tk == tile and tile % cb == 0
    nq = s_len // tq
    cast_in, cast_out, cast_shapes = [], [], []
    for w, layer in stacked_weights:
        _, n_rows, n_cols = w.shape
        rb = _cast_block_rows(n_rows, n_heads * nq)
        assert rb is not None
        blk = lambda h, i, last=n_rows // rb - 1: jnp.minimum(h * nq + i, last)
        cast_in.append(pl.BlockSpec((1, rb, n_cols), lambda h, i, blk=blk, layer=layer: (layer, blk(h, i), 0)))
        cast_out.append(pl.BlockSpec((rb, n_cols), lambda h, i, blk=blk: (blk(h, i), 0)))
        cast_shapes.append(jax.ShapeDtypeStruct((n_rows, n_cols), BF16))
    outs = pl.pallas_call(
        functools.partial(_attn_kernel, tq=tq, tk=tk, cb=cb, vd=vd, n_cast=len(stacked_weights)),
        grid=(n_heads, nq),
        in_specs=[
            pl.BlockSpec((1, tq // tile, hd, tile), lambda h, i: (h, i, 0, 0)),
            pl.BlockSpec((1, s_len, hd), lambda h, i: (h, 0, 0)),
            pl.BlockSpec((1, n_tiles, vda, tile), lambda h, i: (h, 0, 0, 0)),
        ] + cast_in,
        out_specs=[pl.BlockSpec((tq, vd), lambda h, i: (i, h))] + cast_out,
        out_shape=[jax.ShapeDtypeStruct((s_len, n_heads * vd), BF16)] + cast_shapes,
        scratch_shapes=[
            pltpu.VMEM((tk, tq), F32),
            pltpu.VMEM((tk, tq), F32),
            pltpu.VMEM((1, tq), F32),
            pltpu.VMEM((1, tq), F32),
            pltpu.VMEM((vda, tq), F32),
            pltpu.VMEM((1, tq), F32),
        ],
        compiler_params=_params("arbitrary", "arbitrary"),
        name="mla_attention",
    )(qt, k, vt, *[w for w, _ in stacked_weights])
    return outs[0], outs[1:]


def _oproj_kernel(a_ref, w_ref, r_ref, o_ref):
    o_ref[...] = r_ref[...] + jnp.dot(a_ref[...], w_ref[...], preferred_element_type=F32)


def _out_proj(a_bf16, w_bf16, resid, *, tm=512, tn=1024):
    s_len, kdim = a_bf16.shape
    d = w_bf16.shape[1]
    tn = min(tn, d)
    assert s_len % tm == 0 and d % tn == 0
    return pl.pallas_call(
        _oproj_kernel,
        grid=(s_len // tm, d // tn),
        in_specs=[
            pl.BlockSpec((tm, kdim), lambda i, j: (i, 0)),
            pl.BlockSpec((kdim, tn), lambda i, j: (0, j)),
            pl.BlockSpec((tm, tn), lambda i, j: (i, j)),
        ],
        out_specs=pl.BlockSpec((tm, tn), lambda i, j: (i, j)),
        out_shape=jax.ShapeDtypeStruct((s_len, d), F32),
        compiler_params=_params("parallel", "parallel"),
        name="attn_out_proj",
    )(a_bf16, w_bf16, resid)


def _trunk(x, a_norm, a_pool_w, a_scale, kv_norm, w_dkv, kv_lat_norm, w_ukv, b_norm, w_dq, q_lat_norm,
           w_uq, w_o, ffn_norm, ffn_w_in, ffn_conv_w, ffn_conv_b, ffn_w_out, final_norm,
           *, pool_ts=256, ffn_tm=512, ffn_tf=256, proj_tm=512, attn_tq=2048, attn_tk=512, attn_cb=256,
           oproj_tm=512):
    batch, s_len, d = x.shape
    depth = ffn_norm.shape[0]
    n_a = a_norm.shape[0]
    n_b = b_norm.shape[0]
    assert n_a + n_b == depth and n_a >= 1
    lora, n_heads, kvd = w_ukv.shape
    hd = w_uq.shape[-1]
    rope = w_dkv.shape[1] - lora
    nope = hd - rope
    vd = kvd - nope
    half = rope // 2
    hd_pad = -(-hd // LANES) * LANES

    inv_freq = ROPE_THETA ** (-jnp.arange(0, rope, 2, dtype=F32) / rope)
    ang = jnp.arange(s_len, dtype=F32)[:, None] * inv_freq[None, :]
    cos, sin = jnp.cos(ang), jnp.sin(ang)
    cos2 = jnp.concatenate([cos, cos], axis=1)
    sin2 = jnp.concatenate([-sin, sin], axis=1)
    cos_t, sin_t = cos.T, sin.T

    bf = lambda w: w.astype(BF16)
    wd_ext = bf(jnp.concatenate([w_dkv, w_dkv[:, lora + half:], w_dkv[:, lora:lora + half]], axis=1))
    wuk = bf(w_ukv[:, :, :nope].reshape(lora, n_heads * nope))
    wuvt = bf(w_ukv[:, :, nope:].reshape(lora, n_heads * vd).T)
    final_g = final_norm

    outs = []
    ffn_bf16 = {}
    wo_bf16 = {}
    for b in range(batch):
        h = x.reshape(s_len, d) if batch == 1 else x[b]
        shared = None
        for layer in range(depth):
            last = layer == depth - 1
            if layer < n_a:
                h = _pool_layer(h, a_norm[layer], bf(a_pool_w[layer]), a_scale[layer], ts=pool_ts)
            else:
                j = layer - n_a
                qlora = w_dq.shape[2]
                qt = _q_proj(h, b_norm[j], bf(w_dq[j].T), q_lat_norm[j], cos_t, sin_t,
                             bf(w_uq[j].reshape(qlora, n_heads * hd).T),
                             n_heads=n_heads, nope=nope, rope=rope, hd_pad=hd_pad,
                             q_scale=hd ** -0.5 * math.log2(math.e), tm=proj_tm)
                first = layer not in ffn_bf16
                to_cast = ((ffn_w_in, layer), (ffn_w_out, layer), (w_o, j)) if first else ()
                o, cast = _attention(qt, shared[0], shared[1], to_cast, vd=vd, tq=attn_tq, tk=attn_tk,
                                     cb=attn_cb)
                if first:
                    ffn_bf16[layer] = cast[:2]
                    wo_bf16[j] = cast[2]
                h = _out_proj(o, wo_bf16[j], h, tm=oproj_tm)
            if layer not in ffn_bf16:
                ffn_bf16[layer] = (bf(ffn_w_in[layer]), bf(ffn_w_out[layer]))
            h = _ffn_layer(h, ffn_norm[layer], ffn_bf16[layer][0], ffn_conv_w[layer], ffn_conv_b[layer],
                           ffn_bf16[layer][1], final_g, final=last, tm=ffn_tm, tf=ffn_tf)
            if layer == n_a - 1:
                shared = _kv_proj(h, kv_norm, wd_ext, kv_lat_norm, cos2, sin2, wuk, wuvt,
                                  n_heads=n_heads, nope=nope, rope=rope, hd_pad=hd_pad,
                                  ones_rows=BF16_SUBLANES, tm=proj_tm)
        outs.append(h)
    return outs[0].reshape(1, s_len, d) if batch == 1 else jnp.stack(outs, axis=0)


def kernel(x, a_norm, a_pool_w, a_scale, kv_norm, w_dkv, kv_lat_norm, w_ukv, b_norm, w_dq, q_lat_norm,
           w_uq, w_o, ffn_norm, ffn_w_in, ffn_conv_w, ffn_conv_b, ffn_w_out, final_norm):
    return _trunk(x, a_norm, a_pool_w, a_scale, kv_norm, w_dkv, kv_lat_norm, w_ukv, b_norm, w_dq,
                  q_lat_norm, w_uq, w_o, ffn_norm, ffn_w_in, ffn_conv_w, ffn_conv_b, ffn_w_out,
                  final_norm)
```

```python
import functools
import math

import jax
import jax.numpy as jnp
from jax import lax
from jax.experimental import pallas as pl
from jax.experimental.pallas import tpu as pltpu

EPS = 1e-6
ROPE_THETA = 10000.0
POOL_WINDOWS = (2, 4, 8, 16)
CONV_WIDTH = 3
POOL_HALO = 16
CONV_HALO = 8
NORM_PIECES = 4
CONV_PIECES = 16
MASK_VALUE = -1e30
VMEM_LIMIT_BYTES = 56 * 1024 * 1024
LANES = 128
BF16_SUBLANES = 16

F32 = jnp.float32
BF16 = jnp.bfloat16


def _rms(x, g):
    return x * lax.rsqrt(jnp.mean(x * x, axis=-1, keepdims=True) + EPS) * g


def _params(*sem):
    return pltpu.CompilerParams(dimension_semantics=sem, vmem_limit_bytes=VMEM_LIMIT_BYTES)


def _const_spec(shape):
    nd = len(shape)
    return pl.BlockSpec(shape, lambda *_: (0,) * nd, pipeline_mode=pl.Buffered(1))


def _pool_kernel(x_ref, halo_ref, g_ref, w_ref, sc_ref, o_ref, *, ts, gc):
    i = pl.program_id(0)
    g = g_ref[...]
    x = x_ref[...]
    xn = _rms(x, g)
    hn = jnp.where(i == 0, 0.0, _rms(halo_ref[...], g))
    pos = i * ts + lax.broadcasted_iota(jnp.int32, (ts, 1), 0)
    for gi, w in enumerate(POOL_WINDOWS):
        sl = slice(gi * gc, (gi + 1) * gc)
        xg = xn[:, sl]
        s = jnp.concatenate([hn[:, sl], xg], axis=0)
        k = 1
        while k < w:
            s = s + pltpu.roll(s, k, axis=0)
            k *= 2
        cnt = jnp.minimum(pos + 1, w).astype(F32)
        pooled = s[POOL_HALO:] / cnt - xg
        mixed = jnp.dot(pooled.astype(BF16), w_ref[gi], preferred_element_type=F32)
        o_ref[:, sl] = x[:, sl] + mixed * sc_ref[:, sl]


def _pool_layer(x, norm_g, w_bf16, scale, *, ts=256):
    s_len, d = x.shape
    ng, gc, _ = w_bf16.shape
    assert ng == len(POOL_WINDOWS) and ng * gc == d and s_len % ts == 0 and ts % POOL_HALO == 0
    per = ts // POOL_HALO
    return pl.pallas_call(
        functools.partial(_pool_kernel, ts=ts, gc=gc),
        grid=(s_len // ts,),
        in_specs=[
            pl.BlockSpec((ts, d), lambda i: (i, 0)),
            pl.BlockSpec((POOL_HALO, d), lambda i: (jnp.maximum(i * per - 1, 0), 0)),
            _const_spec((1, d)),
            _const_spec((ng, gc, gc)),
            _const_spec((1, d)),
        ],
        out_specs=pl.BlockSpec((ts, d), lambda i: (i, 0)),
        out_shape=jax.ShapeDtypeStruct((s_len, d), F32),
        compiler_params=_params("parallel"),
        name="pool_layer",
    )(x, x, norm_g.reshape(1, d), w_bf16, scale.reshape(1, d))


def _ffn_kernel(h_ref, g_ref, wg_ref, wv_ref, cwg_ref, cwv_ref, bg_ref, bv_ref, wo_ref, fg_ref,
                o_ref, xn_s, a0_s, a1_s, ug0_s, ug1_s, uv0_s, uv1_s, cg_s, cv_s, *, tm, tn, nf, final):
    i = pl.program_id(0)
    f = pl.program_id(1)
    acts = (a0_s, a1_s)
    ugs = (ug0_s, ug1_s)
    uvs = (uv0_s, uv1_s)

    c = jnp.clip(f - 1, 0, nf - 1)

    @pl.when((i == 0) & (f >= 1) & (f <= nf))
    def _():
        cg_s[c] = jnp.zeros(cg_s.shape[1:], F32)
        cv_s[c] = jnp.zeros(cv_s.shape[1:], F32)

    n_down = o_ref.shape[1] // tn
    rows = tm // CONV_PIECES

    def up_piece(p, which, half):
        u_s, w_ref = ((ugs[p], wg_ref), (uvs[p], wv_ref))[which]
        r = slice(half * (tm // 2), (half + 1) * (tm // 2))
        u_s[r] = jnp.dot(xn_s[r], w_ref[...], preferred_element_type=F32)

    def conv_rows(u_s, cw_ref, b_ref, c_s, r0):
        prev = c_s[c] if r0 == 0 else u_s[r0 - CONV_HALO:r0]
        u = u_s[r0:r0 + rows]
        cat = jnp.concatenate([prev, u], axis=0)
        u1 = pltpu.roll(cat, 1, axis=0)[CONV_HALO:]
        u2 = pltpu.roll(cat, 2, axis=0)[CONV_HALO:]
        cw = cw_ref[...]
        return cw[0:1] * u2 + cw[1:2] * u1 + cw[2:3] * u + b_ref[...]

    def conv_piece(p, k):
        r0 = k * rows
        gate = conv_rows(ugs[p], cwg_ref, bg_ref, cg_s, r0)
        val = conv_rows(uvs[p], cwv_ref, bv_ref, cv_s, r0)
        acts[p][r0:r0 + rows] = (gate * (1.0 / (1.0 + jnp.exp(-gate))) * val).astype(BF16)
        if k == CONV_PIECES - 1:
            cg_s[c] = ugs[p][tm - CONV_HALO:]
            cv_s[c] = uvs[p][tm - CONV_HALO:]

    def down_piece(p, n):
        sl = slice(n * tn, (n + 1) * tn)
        o_ref[:, sl] += jnp.dot(acts[p][...], wo_ref[:, sl], preferred_element_type=F32)

    def run(p_up=None, p_conv=None, p_down=None):
        mxu, vpu = [], []
        if p_up is not None:
            mxu += [functools.partial(up_piece, p_up, w, half) for half in range(2) for w in range(2)]
        if p_down is not None:
            mxu += [functools.partial(down_piece, p_down, n) for n in range(n_down)]
        if p_conv is not None:
            vpu += [functools.partial(conv_piece, p_conv, k) for k in range(CONV_PIECES)]
        for idx in range(max(len(mxu), len(vpu))):
            if idx < len(mxu):
                mxu[idx]()
            if idx < len(vpu):
                vpu[idx]()

    @pl.when(f == 0)
    def _():
        rn = tm // NORM_PIECES

        def norm_rows(k):
            r = slice(k * rn, (k + 1) * rn)
            h = h_ref[r]
            xn_s[r] = _rms(h, g_ref[...]).astype(BF16)
            o_ref[r] = h

        def up_rows(k):
            r = slice(k * rn, (k + 1) * rn)
            ugs[0][r] = jnp.dot(xn_s[r], wg_ref[...], preferred_element_type=F32)
            uvs[0][r] = jnp.dot(xn_s[r], wv_ref[...], preferred_element_type=F32)

        norm_rows(0)
        for k in range(NORM_PIECES):
            if k + 1 < NORM_PIECES:
                norm_rows(k + 1)
            up_rows(k)

    @pl.when(f == 1)
    def _():
        run(p_up=1, p_conv=0)

    for p in range(2):
        @pl.when((f >= 2) & (f < nf) & (f % 2 == p))
        def _():
            run(p_up=p, p_conv=1 - p, p_down=p)

    @pl.when(f == nf)
    def _():
        run(p_conv=(nf - 1) % 2, p_down=nf % 2)

    @pl.when(f == nf + 1)
    def _():
        run(p_down=(nf - 1) % 2)
        if final:
            o_ref[...] = _rms(o_ref[...], fg_ref[...])


def _ffn_layer(h, norm_g, w_in_bf16, conv_w, conv_b, w_out_bf16, final_g, *, final, tm=512, tf=256,
               tn=256):
    s_len, d = h.shape
    dff = w_out_bf16.shape[0]
    tn = min(tn, d)
    assert s_len % tm == 0 and dff % tf == 0 and d % tn == 0 and w_in_bf16.shape == (d, 2 * dff)
    nf = dff // tf
    assert nf >= 2
    conv_b2 = conv_b.reshape(1, 2 * dff)
    up = lambda f: jnp.minimum(f, nf - 1)
    cv = lambda f: jnp.clip(f - 1, 0, nf - 1)
    dn = lambda f: jnp.clip(f - 2, 0, nf - 1)
    return pl.pallas_call(
        functools.partial(_ffn_kernel, tm=tm, tn=tn, nf=nf, final=final),
        grid=(s_len // tm, nf + 2),
        in_specs=[
            pl.BlockSpec((tm, d), lambda i, f: (i, 0), pipeline_mode=pl.Buffered(1)),
            _const_spec((1, d)),
            pl.BlockSpec((d, tf), lambda i, f: (0, up(f))),
            pl.BlockSpec((d, tf), lambda i, f: (0, nf + up(f))),
            pl.BlockSpec((CONV_WIDTH, tf), lambda i, f: (0, cv(f))),
            pl.BlockSpec((CONV_WIDTH, tf), lambda i, f: (0, nf + cv(f))),
            pl.BlockSpec((1, tf), lambda i, f: (0, cv(f))),
            pl.BlockSpec((1, tf), lambda i, f: (0, nf + cv(f))),
            pl.BlockSpec((tf, d), lambda i, f: (dn(f), 0)),
            _const_spec((1, d)),
        ],
        out_specs=pl.BlockSpec((tm, d), lambda i, f: (i, 0)),
        out_shape=jax.ShapeDtypeStruct((s_len, d), F32),
        scratch_shapes=[
            pltpu.VMEM((tm, d), BF16),
            pltpu.VMEM((tm, tf), BF16),
            pltpu.VMEM((tm, tf), BF16),
            pltpu.VMEM((tm, tf), F32),
            pltpu.VMEM((tm, tf), F32),
            pltpu.VMEM((tm, tf), F32),
            pltpu.VMEM((tm, tf), F32),
            pltpu.VMEM((nf, CONV_HALO, tf), F32),
            pltpu.VMEM((nf, CONV_HALO, tf), F32),
        ],
        compiler_params=_params("arbitrary", "arbitrary"),
        name="conv_ffn",
    )(h, norm_g.reshape(1, d), w_in_bf16, w_in_bf16, conv_w, conv_w, conv_b2, conv_b2, w_out_bf16,
      final_g.reshape(1, d))


def _kv_kernel(h_ref, g_ref, wd_ref, lat_ref, cos_ref, sin_ref, wuk_ref, wuvt_ref,
               k_ref, vt_ref, c_s, ct_s, kr_s, *, lora, rope, nope, vd, hg):
    gidx = pl.program_id(1)

    @pl.when(gidx == 0)
    def _():
        hn = _rms(h_ref[...], g_ref[...]).astype(BF16)
        ckv = jnp.dot(hn, wd_ref[...], preferred_element_type=F32)
        c = _rms(ckv[:, :lora], lat_ref[...])
        c_s[...] = c.astype(BF16)
        ct_s[...] = c.T.astype(BF16)
        kr = ckv[:, lora:lora + rope] * cos_ref[...] + ckv[:, lora + rope:] * sin_ref[...]
        kr_s[...] = kr.astype(BF16)

    wuk = wuk_ref[:, pl.ds(pl.multiple_of(gidx * (hg * nope), hg * nope), hg * nope)]
    wuvt = wuvt_ref[pl.ds(pl.multiple_of(gidx * (hg * vd), hg * vd), hg * vd), :]
    kn = jnp.dot(c_s[...], wuk, preferred_element_type=F32)
    vt = jnp.dot(wuvt, ct_s[...], preferred_element_type=F32)
    tm = kn.shape[0]
    hd = nope + rope
    k_pad = k_ref.shape[2] - hd
    ones_rows = vt_ref.shape[2] - vd
    for hh in range(hg):
        k_ref[hh, :, 0:nope] = kn[:, hh * nope:(hh + 1) * nope].astype(BF16)
        k_ref[hh, :, nope:hd] = kr_s[...]
        if k_pad:
            k_ref[hh, :, hd:hd + k_pad] = jnp.zeros((tm, k_pad), BF16)
        vt_ref[hh, 0, 0:vd] = vt[hh * vd:(hh + 1) * vd].astype(BF16)
        vt_ref[hh, 0, vd:vd + ones_rows] = jnp.ones((ones_rows, tm), BF16)


def _kv_proj(h, norm_g, wd_ext_bf16, lat_g, cos2, sin2, wuk_bf16, wuvt_bf16, *, n_heads, nope, rope,
             hd_pad, ones_rows, tile, tm=256, hg=32):
    s_len, d = h.shape
    lora = wuk_bf16.shape[0]
    vd = wuvt_bf16.shape[0] // n_heads
    hg = min(hg, n_heads)
    tm = min(tm, tile)
    per = tile // tm
    assert s_len % tile == 0 and tile % tm == 0 and n_heads % hg == 0
    assert wd_ext_bf16.shape == (d, lora + 2 * rope)
    return pl.pallas_call(
        functools.partial(_kv_kernel, lora=lora, rope=rope, nope=nope, vd=vd, hg=hg),
        grid=(s_len // tm, n_heads // hg),
        in_specs=[
            pl.BlockSpec((tm, d), lambda i, g: (i, 0)),
            _const_spec((1, d)),
            _const_spec((d, lora + 2 * rope)),
            _const_spec((1, lora)),
            pl.BlockSpec((tm, rope), lambda i, g: (i, 0)),
            pl.BlockSpec((tm, rope), lambda i, g: (i, 0)),
            _const_spec((lora, n_heads * nope)),
            _const_spec((n_heads * vd, lora)),
        ],
        out_specs=[
            pl.BlockSpec((hg, tm, hd_pad), lambda i, g: (g, i, 0)),
            pl.BlockSpec((hg, 1, vd + ones_rows, tm), lambda i, g: (g, i // per, 0, i % per)),
        ],
        out_shape=[
            jax.ShapeDtypeStruct((n_heads, s_len, hd_pad), BF16),
            jax.ShapeDtypeStruct((n_heads, s_len // tile, vd + ones_rows, tile), BF16),
        ],
        scratch_shapes=[
            pltpu.VMEM((tm, lora), BF16),
            pltpu.VMEM((lora, tm), BF16),
            pltpu.VMEM((tm, rope), BF16),
        ],
        compiler_params=_params("parallel", "arbitrary"),
        name="kv_proj",
    )(h, norm_g.reshape(1, d), wd_ext_bf16, lat_g.reshape(1, lora), cos2, sin2, wuk_bf16, wuvt_bf16)


def _q_kernel(h_ref, g_ref, wdt_ref, lat_ref, cos_ref, sin_ref, wut_ref, qt_ref, cqt_s,
              *, nope, rope, hg, q_scale):
    gidx = pl.program_id(1)

    @pl.when(gidx == 0)
    def _():
        xn = _rms(h_ref[...], g_ref[...]).astype(BF16)
        cqt = lax.dot_general(wdt_ref[...], xn, (((1,), (1,)), ((), ())),
                              preferred_element_type=F32)
        ms = jnp.mean(cqt * cqt, axis=0, keepdims=True)
        cqt_s[...] = (cqt * lax.rsqrt(ms + EPS) * lat_ref[...]).astype(BF16)

    rows = hg * (nope + rope)
    wut = wut_ref[pl.ds(pl.multiple_of(gidx * rows, rows), rows), :]
    qt = jnp.dot(wut, cqt_s[...], preferred_element_type=F32) * q_scale
    cos_t = cos_ref[...]
    sin_t = sin_ref[...]
    hd = nope + rope
    half = rope // 2
    pad = qt_ref.shape[2] - hd
    for hh in range(hg):
        b = hh * hd
        qt_ref[hh, 0, 0:nope] = qt[b:b + nope].astype(BF16)
        x1 = qt[b + nope:b + nope + half]
        x2 = qt[b + nope + half:b + hd]
        qt_ref[hh, 0, nope:nope + half] = (x1 * cos_t - x2 * sin_t).astype(BF16)
        qt_ref[hh, 0, nope + half:hd] = (x2 * cos_t + x1 * sin_t).astype(BF16)
        if pad:
            qt_ref[hh, 0, hd:hd + pad] = jnp.zeros((pad, qt.shape[1]), BF16)


def _q_proj(h, norm_g, wdqt_bf16, lat_g, cos_t, sin_t, wuqt_bf16, *, n_heads, nope, rope, hd_pad,
            q_scale, tile, tm=512, hg=16):
    s_len, d = h.shape
    qlora = wdqt_bf16.shape[0]
    hd = nope + rope
    hg = min(hg, n_heads)
    tm = min(tm, tile)
    per = tile // tm
    assert s_len % tile == 0 and tile % tm == 0 and n_heads % hg == 0
    assert wuqt_bf16.shape == (n_heads * hd, qlora)
    return pl.pallas_call(
        functools.partial(_q_kernel, nope=nope, rope=rope, hg=hg, q_scale=q_scale),
        grid=(s_len // tm, n_heads // hg),
        in_specs=[
            pl.BlockSpec((tm, d), lambda i, g: (i, 0)),
            _const_spec((1, d)),
            _const_spec((qlora, d)),
            _const_spec((qlora, 1)),
            pl.BlockSpec((rope // 2, tm), lambda i, g: (0, i)),
            pl.BlockSpec((rope // 2, tm), lambda i, g: (0, i)),
            _const_spec((n_heads * hd, qlora)),
        ],
        out_specs=pl.BlockSpec((hg, 1, hd_pad, tm), lambda i, g: (g, i // per, 0, i % per)),
        out_shape=jax.ShapeDtypeStruct((n_heads, s_len // tile, hd_pad, tile), BF16),
        scratch_shapes=[pltpu.VMEM((qlora, tm), BF16)],
        compiler_params=_params("parallel", "arbitrary"),
        name="q_proj",
    )(h, norm_g.reshape(1, d), wdqt_bf16, lat_g.reshape(qlora, 1), cos_t, sin_t, wuqt_bf16)


def _attn_kernel(qt_ref, k_ref, vt_ref, *rest, tq, tk, cb, vd, n_cast):
    w_f32 = rest[:n_cast]
    o_ref = rest[n_cast]
    w_bf16 = rest[n_cast + 1:2 * n_cast + 1]
    s0_ref, s1_ref, cmax0_ref, cmax1_ref, acc_ref, m_ref = rest[2 * n_cast + 1:]
    cmax_buf = (cmax0_ref, cmax1_ref)
    for src, dst in zip(w_f32, w_bf16):
        dst[...] = src[0].astype(BF16)

    i = pl.program_id(1)
    tile = qt_ref.shape[3]
    r = tq // tk
    s_buf = (s0_ref, s1_ref)

    def scores(j, slot, c0, c1):
        kblk = k_ref[0, pl.ds(pl.multiple_of(j * tk, tk), tk), :]
        qt = qt_ref[0, c0 // tile, :, c0 % tile:c0 % tile + (c1 - c0)]
        s = jnp.dot(kblk, qt, preferred_element_type=F32)
        s_buf[slot][:, c0:c1] = s
        cmax_buf[slot][:, c0:c1] = jnp.max(s, axis=0, keepdims=True)

    def weights_values(j, slot, c0, c1, key_offset):
        s = s_buf[slot][:, c0:c1]
        if key_offset is not None and key_offset + tk - 1 > c0:
            kpos = lax.broadcasted_iota(jnp.int32, s.shape, 0) + key_offset
            qpos = lax.broadcasted_iota(jnp.int32, s.shape, 1) + c0
            s = jnp.where(kpos <= qpos, s, MASK_VALUE)
            cmax = jnp.max(s, axis=0, keepdims=True)
        else:
            cmax = cmax_buf[slot][:, c0:c1]
        m = m_ref[:, c0:c1]
        m_new = jnp.maximum(m, cmax)
        alpha = jnp.exp2(m - m_new)
        p = jnp.exp2(s - m_new)
        m_ref[:, c0:c1] = m_new
        vt = vt_ref[0, j]
        acc_ref[:, c0:c1] = alpha * acc_ref[:, c0:c1] + jnp.dot(vt, p.astype(BF16),
                                                               preferred_element_type=F32)

    def col_blocks(lo):
        return [(c0, min(c0 + cb, tq)) for c0 in range(lo, tq, cb)]

    def stage(j, slot, lo=0, diagonal=False, next_lo=0):
        cur = col_blocks(lo)
        nxt = col_blocks(next_lo) if next_lo is not None else []
        for idx in range(max(len(cur), len(nxt))):
            if idx < len(nxt):
                scores(j + 1, 1 - slot, *nxt[idx])
            if idx < len(cur):
                weights_values(j, slot, *cur[idx], key_offset=lo if diagonal else None)

    m_ref[...] = jnp.full(m_ref.shape, MASK_VALUE, F32)
    acc_ref[...] = jnp.zeros(acc_ref.shape, F32)
    for c0, c1 in col_blocks(0):
        scores(0, 0, c0, c1)

    def group(g, carry):
        for c in range(r):
            stage(g * r + c, c % 2)
        return carry

    lax.fori_loop(0, i, group, 0)
    for c in range(r):
        stage(i * r + c, c % 2, lo=c * tk, diagonal=True, next_lo=(c + 1) * tk if c < r - 1 else None)
    o_ref[...] = (acc_ref[0:vd] / acc_ref[vd:vd + 1]).T.astype(o_ref.dtype)


def _cast_block_rows(n_rows, n_steps):
    for rb in range(BF16_SUBLANES, n_rows + 1, BF16_SUBLANES):
        if n_rows % rb == 0 and n_rows // rb <= n_steps:
            return rb
    return None


def _attention(qt, k, vt, stacked_weights, *, vd, tq=2048, tk=512, cb=512):
    n_heads, n_tiles, hd, tile = qt.shape
    s_len = n_tiles * tile
    vda = vt.shape[2]
    tq = min(tq, s_len)
    cb = min(cb, tq)
    assert k.shape == (n_heads, s_len, hd) and vt.shape == (n_heads, n_tiles, vda, tile) and vda > vd
    assert s_len % tq == 0 and tq % (2 * tk) == 0 and tk == tile and tile % cb == 0
    nq = s_len // tq
    cast_in, cast_out, cast_shapes = [], [], []
    for w, layer in stacked_weights:
        _, n_rows, n_cols = w.shape
        rb = _cast_block_rows(n_rows, n_heads * nq)
        assert rb is not None
        blk = lambda h, i, last=n_rows // rb - 1: jnp.minimum(h * nq + i, last)
        cast_in.append(pl.BlockSpec((1, rb, n_cols), lambda h, i, blk=blk, layer=layer: (layer, blk(h, i), 0)))
        cast_out.append(pl.BlockSpec((rb, n_cols), lambda h, i, blk=blk: (blk(h, i), 0)))
        cast_shapes.append(jax.ShapeDtypeStruct((n_rows, n_cols), BF16))
    outs = pl.pallas_call(
        functools.partial(_attn_kernel, tq=tq, tk=tk, cb=cb, vd=vd, n_cast=len(stacked_weights)),
        grid=(n_heads, nq),
        in_specs=[
            pl.BlockSpec((1, tq // tile, hd, tile), lambda h, i: (h, i, 0, 0)),
            pl.BlockSpec((1, s_len, hd), lambda h, i: (h, 0, 0)),
            pl.BlockSpec((1, n_tiles, vda, tile), lambda h, i: (h, 0, 0, 0)),
        ] + cast_in,
        out_specs=[pl.BlockSpec((tq, vd), lambda h, i: (i, h))] + cast_out,
        out_shape=[jax.ShapeDtypeStruct((s_len, n_heads * vd), BF16)] + cast_shapes,
        scratch_shapes=[
            pltpu.VMEM((tk, tq), F32),
            pltpu.VMEM((tk, tq), F32),
            pltpu.VMEM((1, tq), F32),
            pltpu.VMEM((1, tq), F32),
            pltpu.VMEM((vda, tq), F32),
            pltpu.VMEM((1, tq), F32),
        ],
        compiler_params=_params("arbitrary", "arbitrary"),
        name="mla_attention",
    )(qt, k, vt, *[w for w, _ in stacked_weights])
    return outs[0], outs[1:]


def _oproj_kernel(a_ref, w_ref, r_ref, o_ref):
    o_ref[...] = r_ref[...] + jnp.dot(a_ref[...], w_ref[...], preferred_element_type=F32)


def _out_proj(a_bf16, w_bf16, resid, *, tm=512, tn=1024):
    s_len, kdim = a_bf16.shape
    d = w_bf16.shape[1]
    tn = min(tn, d)
    assert s_len % tm == 0 and d % tn == 0
    return pl.pallas_call(
        _oproj_kernel,
        grid=(s_len // tm, d // tn),
        in_specs=[
            pl.BlockSpec((tm, kdim), lambda i, j: (i, 0)),
            pl.BlockSpec((kdim, tn), lambda i, j: (0, j)),
            pl.BlockSpec((tm, tn), lambda i, j: (i, j)),
        ],
        out_specs=pl.BlockSpec((tm, tn), lambda i, j: (i, j)),
        out_shape=jax.ShapeDtypeStruct((s_len, d), F32),
        compiler_params=_params("parallel", "parallel"),
        name="attn_out_proj",
    )(a_bf16, w_bf16, resid)


def _trunk(x, a_norm, a_pool_w, a_scale, kv_norm, w_dkv, kv_lat_norm, w_ukv, b_norm, w_dq, q_lat_norm,
           w_uq, w_o, ffn_norm, ffn_w_in, ffn_conv_w, ffn_conv_b, ffn_w_out, final_norm,
           *, pool_ts=256, ffn_tm=512, ffn_tf=256, proj_tm=512, kv_tm=256, attn_tq=2048, attn_tk=512, attn_cb=256,
           oproj_tm=512):
    batch, s_len, d = x.shape
    depth = ffn_norm.shape[0]
    n_a = a_norm.shape[0]
    n_b = b_norm.shape[0]
    assert n_a + n_b == depth and n_a >= 1
    lora, n_heads, kvd = w_ukv.shape
    hd = w_uq.shape[-1]
    rope = w_dkv.shape[1] - lora
    nope = hd - rope
    vd = kvd - nope
    half = rope // 2
    hd_pad = -(-hd // LANES) * LANES

    inv_freq = ROPE_THETA ** (-jnp.arange(0, rope, 2, dtype=F32) / rope)
    ang = jnp.arange(s_len, dtype=F32)[:, None] * inv_freq[None, :]
    cos, sin = jnp.cos(ang), jnp.sin(ang)
    cos2 = jnp.concatenate([cos, cos], axis=1)
    sin2 = jnp.concatenate([-sin, sin], axis=1)
    cos_t, sin_t = cos.T, sin.T

    bf = lambda w: w.astype(BF16)
    wd_ext = bf(jnp.concatenate([w_dkv, w_dkv[:, lora + half:], w_dkv[:, lora:lora + half]], axis=1))
    wuk = bf(w_ukv[:, :, :nope].reshape(lora, n_heads * nope))
    wuvt = bf(w_ukv[:, :, nope:].reshape(lora, n_heads * vd).T)
    final_g = final_norm

    outs = []
    ffn_bf16 = {}
    wo_bf16 = {}
    for b in range(batch):
        h = x.reshape(s_len, d) if batch == 1 else x[b]
        shared = None
        for layer in range(depth):
            last = layer == depth - 1
            if layer < n_a:
                h = _pool_layer(h, a_norm[layer], bf(a_pool_w[layer]), a_scale[layer], ts=pool_ts)
            else:
                j = layer - n_a
                qlora = w_dq.shape[2]
                qt = _q_proj(h, b_norm[j], bf(w_dq[j].T), q_lat_norm[j], cos_t, sin_t,
                             bf(w_uq[j].reshape(qlora, n_heads * hd).T),
                             n_heads=n_heads, nope=nope, rope=rope, hd_pad=hd_pad,
                             q_scale=hd ** -0.5 * math.log2(math.e), tile=proj_tm, tm=proj_tm)
                first = layer not in ffn_bf16
                to_cast = ((ffn_w_in, layer), (ffn_w_out, layer), (w_o, j)) if first else ()
                o, cast = _attention(qt, shared[0], shared[1], to_cast, vd=vd, tq=attn_tq, tk=attn_tk,
                                     cb=attn_cb)
                if first:
                    ffn_bf16[layer] = cast[:2]
                    wo_bf16[j] = cast[2]
                h = _out_proj(o, wo_bf16[j], h, tm=oproj_tm)
            if layer not in ffn_bf16:
                ffn_bf16[layer] = (bf(ffn_w_in[layer]), bf(ffn_w_out[layer]))
            h = _ffn_layer(h, ffn_norm[layer], ffn_bf16[layer][0], ffn_conv_w[layer], ffn_conv_b[layer],
                           ffn_bf16[layer][1], final_g, final=last, tm=ffn_tm, tf=ffn_tf)
            if layer == n_a - 1:
                shared = _kv_proj(h, kv_norm, wd_ext, kv_lat_norm, cos2, sin2, wuk, wuvt,
                                  n_heads=n_heads, nope=nope, rope=rope, hd_pad=hd_pad,
                                  ones_rows=BF16_SUBLANES, tile=proj_tm, tm=kv_tm)
        outs.append(h)
    return outs[0].reshape(1, s_len, d) if batch == 1 else jnp.stack(outs, axis=0)


def kernel(x, a_norm, a_pool_w, a_scale, kv_norm, w_dkv, kv_lat_norm, w_ukv, b_norm, w_dq, q_lat_norm,
           w_uq, w_o, ffn_norm, ffn_w_in, ffn_conv_w, ffn_conv_b, ffn_w_out, final_norm):
    return _trunk(x, a_norm, a_pool_w, a_scale, kv_norm, w_dkv, kv_lat_norm, w_ukv, b_norm, w_dq,
                  q_lat_norm, w_uq, w_o, ffn_norm, ffn_w_in, ffn_conv_w, ffn_conv_b, ffn_w_out,
                  final_norm)
```

```python
import functools
import math

import jax
import jax.numpy as jnp
from jax import lax
from jax.experimental import pallas as pl
from jax.experimental.pallas import tpu as pltpu

EPS = 1e-6
ROPE_THETA = 10000.0
POOL_WINDOWS = (2, 4, 8, 16)
CONV_WIDTH = 3
POOL_HALO = 16
CONV_HALO = 8
NORM_PIECES = 4
CONV_PIECES = 16
MASK_VALUE = -1e30
VMEM_LIMIT_BYTES = 56 * 1024 * 1024
LANES = 128
BF16_SUBLANES = 16

F32 = jnp.float32
BF16 = jnp.bfloat16


def _rms(x, g):
    return x * lax.rsqrt(jnp.mean(x * x, axis=-1, keepdims=True) + EPS) * g


def _params(*sem):
    return pltpu.CompilerParams(dimension_semantics=sem, vmem_limit_bytes=VMEM_LIMIT_BYTES)


def _const_spec(shape):
    nd = len(shape)
    return pl.BlockSpec(shape, lambda *_: (0,) * nd, pipeline_mode=pl.Buffered(1))


def _pool_kernel(x_ref, halo_ref, g_ref, w_ref, sc_ref, o_ref, *, ts, gc):
    i = pl.program_id(0)
    g = g_ref[...]
    x = x_ref[...]
    xn = _rms(x, g)
    hn = jnp.where(i == 0, 0.0, _rms(halo_ref[...], g))
    pos = i * ts + lax.broadcasted_iota(jnp.int32, (ts, 1), 0)
    for gi, w in enumerate(POOL_WINDOWS):
        sl = slice(gi * gc, (gi + 1) * gc)
        xg = xn[:, sl]
        s = jnp.concatenate([hn[:, sl], xg], axis=0)
        k = 1
        while k < w:
            s = s + pltpu.roll(s, k, axis=0)
            k *= 2
        cnt = jnp.minimum(pos + 1, w).astype(F32)
        pooled = s[POOL_HALO:] / cnt - xg
        mixed = jnp.dot(pooled.astype(BF16), w_ref[gi], preferred_element_type=F32)
        o_ref[:, sl] = x[:, sl] + mixed * sc_ref[:, sl]


def _pool_layer(x, norm_g, w_bf16, scale, *, ts=256):
    s_len, d = x.shape
    ng, gc, _ = w_bf16.shape
    assert ng == len(POOL_WINDOWS) and ng * gc == d and s_len % ts == 0 and ts % POOL_HALO == 0
    per = ts // POOL_HALO
    return pl.pallas_call(
        functools.partial(_pool_kernel, ts=ts, gc=gc),
        grid=(s_len // ts,),
        in_specs=[
            pl.BlockSpec((ts, d), lambda i: (i, 0)),
            pl.BlockSpec((POOL_HALO, d), lambda i: (jnp.maximum(i * per - 1, 0), 0)),
            _const_spec((1, d)),
            _const_spec((ng, gc, gc)),
            _const_spec((1, d)),
        ],
        out_specs=pl.BlockSpec((ts, d), lambda i: (i, 0)),
        out_shape=jax.ShapeDtypeStruct((s_len, d), F32),
        compiler_params=_params("parallel"),
        name="pool_layer",
    )(x, x, norm_g.reshape(1, d), w_bf16, scale.reshape(1, d))


def _ffn_kernel(h_ref, g_ref, wg_ref, wv_ref, cwg_ref, cwv_ref, bg_ref, bv_ref, wo_ref, fg_ref,
                o_ref, xn_s, a0_s, a1_s, ug0_s, ug1_s, uv0_s, uv1_s, cg_s, cv_s, *, tm, tn, nf, final):
    i = pl.program_id(0)
    f = pl.program_id(1)
    acts = (a0_s, a1_s)
    ugs = (ug0_s, ug1_s)
    uvs = (uv0_s, uv1_s)

    c = jnp.clip(f - 1, 0, nf - 1)

    @pl.when((i == 0) & (f >= 1) & (f <= nf))
    def _():
        cg_s[c] = jnp.zeros(cg_s.shape[1:], F32)
        cv_s[c] = jnp.zeros(cv_s.shape[1:], F32)

    n_down = o_ref.shape[1] // tn
    rows = tm // CONV_PIECES

    def up_piece(p, which, half):
        u_s, w_ref = ((ugs[p], wg_ref), (uvs[p], wv_ref))[which]
        r = slice(half * (tm // 2), (half + 1) * (tm // 2))
        u_s[r] = jnp.dot(xn_s[r], w_ref[...], preferred_element_type=F32)

    def conv_rows(u_s, cw_ref, b_ref, c_s, r0):
        prev = c_s[c] if r0 == 0 else u_s[r0 - CONV_HALO:r0]
        u = u_s[r0:r0 + rows]
        cat = jnp.concatenate([prev, u], axis=0)
        u1 = pltpu.roll(cat, 1, axis=0)[CONV_HALO:]
        u2 = pltpu.roll(cat, 2, axis=0)[CONV_HALO:]
        cw = cw_ref[...]
        return cw[0:1] * u2 + cw[1:2] * u1 + cw[2:3] * u + b_ref[...]

    def conv_piece(p, k):
        r0 = k * rows
        gate = conv_rows(ugs[p], cwg_ref, bg_ref, cg_s, r0)
        val = conv_rows(uvs[p], cwv_ref, bv_ref, cv_s, r0)
        acts[p][r0:r0 + rows] = (gate * (1.0 / (1.0 + jnp.exp(-gate))) * val).astype(BF16)
        if k == CONV_PIECES - 1:
            cg_s[c] = ugs[p][tm - CONV_HALO:]
            cv_s[c] = uvs[p][tm - CONV_HALO:]

    def down_piece(p, n):
        sl = slice(n * tn, (n + 1) * tn)
        o_ref[:, sl] += jnp.dot(acts[p][...], wo_ref[:, sl], preferred_element_type=F32)

    def run(p_up=None, p_conv=None, p_down=None):
        mxu, vpu = [], []
        if p_up is not None:
            mxu += [functools.partial(up_piece, p_up, w, half) for half in range(2) for w in range(2)]
        if p_down is not None:
            mxu += [functools.partial(down_piece, p_down, n) for n in range(n_down)]
        if p_conv is not None:
            vpu += [functools.partial(conv_piece, p_conv, k) for k in range(CONV_PIECES)]
        for idx in range(max(len(mxu), len(vpu))):
            if idx < len(mxu):
                mxu[idx]()
            if idx < len(vpu):
                vpu[idx]()

    @pl.when(f == 0)
    def _():
        rn = tm // NORM_PIECES

        def norm_rows(k):
            r = slice(k * rn, (k + 1) * rn)
            h = h_ref[r]
            xn_s[r] = _rms(h, g_ref[...]).astype(BF16)
            o_ref[r] = h

        def up_rows(k):
            r = slice(k * rn, (k + 1) * rn)
            ugs[0][r] = jnp.dot(xn_s[r], wg_ref[...], preferred_element_type=F32)
            uvs[0][r] = jnp.dot(xn_s[r], wv_ref[...], preferred_element_type=F32)

        norm_rows(0)
        for k in range(NORM_PIECES):
            if k + 1 < NORM_PIECES:
                norm_rows(k + 1)
            up_rows(k)

    @pl.when(f == 1)
    def _():
        run(p_up=1, p_conv=0)

    for p in range(2):
        @pl.when((f >= 2) & (f < nf) & (f % 2 == p))
        def _():
            run(p_up=p, p_conv=1 - p, p_down=p)

    @pl.when(f == nf)
    def _():
        run(p_conv=(nf - 1) % 2, p_down=nf % 2)

    @pl.when(f == nf + 1)
    def _():
        run(p_down=(nf - 1) % 2)
        if final:
            o_ref[...] = _rms(o_ref[...], fg_ref[...])


def _ffn_layer(h, norm_g, w_in_bf16, conv_w, conv_b, w_out_bf16, final_g, *, final, tm=512, tf=256,
               tn=256):
    s_len, d = h.shape
    dff = w_out_bf16.shape[0]
    tn = min(tn, d)
    assert s_len % tm == 0 and dff % tf == 0 and d % tn == 0 and w_in_bf16.shape == (d, 2 * dff)
    nf = dff // tf
    assert nf >= 2
    conv_b2 = conv_b.reshape(1, 2 * dff)
    up = lambda f: jnp.minimum(f, nf - 1)
    cv = lambda f: jnp.clip(f - 1, 0, nf - 1)
    dn = lambda f: jnp.clip(f - 2, 0, nf - 1)
    return pl.pallas_call(
        functools.partial(_ffn_kernel, tm=tm, tn=tn, nf=nf, final=final),
        grid=(s_len // tm, nf + 2),
        in_specs=[
            pl.BlockSpec((tm, d), lambda i, f: (i, 0), pipeline_mode=pl.Buffered(1)),
            _const_spec((1, d)),
            pl.BlockSpec((d, tf), lambda i, f: (0, up(f))),
            pl.BlockSpec((d, tf), lambda i, f: (0, nf + up(f))),
            pl.BlockSpec((CONV_WIDTH, tf), lambda i, f: (0, cv(f))),
            pl.BlockSpec((CONV_WIDTH, tf), lambda i, f: (0, nf + cv(f))),
            pl.BlockSpec((1, tf), lambda i, f: (0, cv(f))),
            pl.BlockSpec((1, tf), lambda i, f: (0, nf + cv(f))),
            pl.BlockSpec((tf, d), lambda i, f: (dn(f), 0)),
            _const_spec((1, d)),
        ],
        out_specs=pl.BlockSpec((tm, d), lambda i, f: (i, 0)),
        out_shape=jax.ShapeDtypeStruct((s_len, d), F32),
        scratch_shapes=[
            pltpu.VMEM((tm, d), BF16),
            pltpu.VMEM((tm, tf), BF16),
            pltpu.VMEM((tm, tf), BF16),
            pltpu.VMEM((tm, tf), F32),
            pltpu.VMEM((tm, tf), F32),
            pltpu.VMEM((tm, tf), F32),
            pltpu.VMEM((tm, tf), F32),
            pltpu.VMEM((nf, CONV_HALO, tf), F32),
            pltpu.VMEM((nf, CONV_HALO, tf), F32),
        ],
        compiler_params=_params("arbitrary", "arbitrary"),
        name="conv_ffn",
    )(h, norm_g.reshape(1, d), w_in_bf16, w_in_bf16, conv_w, conv_w, conv_b2, conv_b2, w_out_bf16,
      final_g.reshape(1, d))


def _kv_kernel(h_ref, g_ref, wd_ref, lat_ref, cos_ref, sin_ref, wuk_ref, wuvt_ref,
               k_ref, vt_ref, c_s, ct_s, kr_s, *, lora, rope, nope, vd, hg):
    gidx = pl.program_id(1)

    @pl.when(gidx == 0)
    def _():
        hn = _rms(h_ref[...], g_ref[...]).astype(BF16)
        ckv = jnp.dot(hn, wd_ref[...], preferred_element_type=F32)
        c = _rms(ckv[:, :lora], lat_ref[...])
        c_s[...] = c.astype(BF16)
        ct_s[...] = c.T.astype(BF16)
        kr = ckv[:, lora:lora + rope] * cos_ref[...] + ckv[:, lora + rope:] * sin_ref[...]
        kr_s[...] = kr.astype(BF16)

    wuk = wuk_ref[:, pl.ds(pl.multiple_of(gidx * (hg * nope), hg * nope), hg * nope)]
    wuvt = wuvt_ref[pl.ds(pl.multiple_of(gidx * (hg * vd), hg * vd), hg * vd), :]
    kn = jnp.dot(c_s[...], wuk, preferred_element_type=F32)
    vt = jnp.dot(wuvt, ct_s[...], preferred_element_type=F32)
    tm = kn.shape[0]
    hd = nope + rope
    k_pad = k_ref.shape[2] - hd
    ones_rows = vt_ref.shape[2] - vd
    for hh in range(hg):
        k_ref[hh, :, 0:nope] = kn[:, hh * nope:(hh + 1) * nope].astype(BF16)
        k_ref[hh, :, nope:hd] = kr_s[...]
        if k_pad:
            k_ref[hh, :, hd:hd + k_pad] = jnp.zeros((tm, k_pad), BF16)
        vt_ref[hh, 0, 0:vd] = vt[hh * vd:(hh + 1) * vd].astype(BF16)
        vt_ref[hh, 0, vd:vd + ones_rows] = jnp.ones((ones_rows, tm), BF16)


def _kv_proj(h, norm_g, wd_ext_bf16, lat_g, cos2, sin2, wuk_bf16, wuvt_bf16, *, n_heads, nope, rope,
             hd_pad, ones_rows, tile, tm=256, hg=32):
    s_len, d = h.shape
    lora = wuk_bf16.shape[0]
    vd = wuvt_bf16.shape[0] // n_heads
    hg = min(hg, n_heads)
    tm = min(tm, tile)
    per = tile // tm
    assert s_len % tile == 0 and tile % tm == 0 and n_heads % hg == 0
    assert wd_ext_bf16.shape == (d, lora + 2 * rope)
    return pl.pallas_call(
        functools.partial(_kv_kernel, lora=lora, rope=rope, nope=nope, vd=vd, hg=hg),
        grid=(s_len // tm, n_heads // hg),
        in_specs=[
            pl.BlockSpec((tm, d), lambda i, g: (i, 0)),
            _const_spec((1, d)),
            _const_spec((d, lora + 2 * rope)),
            _const_spec((1, lora)),
            pl.BlockSpec((tm, rope), lambda i, g: (i, 0)),
            pl.BlockSpec((tm, rope), lambda i, g: (i, 0)),
            _const_spec((lora, n_heads * nope)),
            _const_spec((n_heads * vd, lora)),
        ],
        out_specs=[
            pl.BlockSpec((hg, tm, hd_pad), lambda i, g: (g, i, 0)),
            pl.BlockSpec((hg, 1, vd + ones_rows, tm), lambda i, g: (g, i // per, 0, i % per)),
        ],
        out_shape=[
            jax.ShapeDtypeStruct((n_heads, s_len, hd_pad), BF16),
            jax.ShapeDtypeStruct((n_heads, s_len // tile, vd + ones_rows, tile), BF16),
        ],
        scratch_shapes=[
            pltpu.VMEM((tm, lora), BF16),
            pltpu.VMEM((lora, tm), BF16),
            pltpu.VMEM((tm, rope), BF16),
        ],
        compiler_params=_params("parallel", "arbitrary"),
        name="kv_proj",
    )(h, norm_g.reshape(1, d), wd_ext_bf16, lat_g.reshape(1, lora), cos2, sin2, wuk_bf16, wuvt_bf16)


def _q_kernel(h_ref, g_ref, wdt_ref, lat_ref, cos_ref, sin_ref, wut_ref, qt_ref, cqt_s,
              *, nope, rope, hg, q_scale):
    gidx = pl.program_id(1)

    @pl.when(gidx == 0)
    def _():
        xn = _rms(h_ref[...], g_ref[...]).astype(BF16)
        cqt = lax.dot_general(wdt_ref[...], xn, (((1,), (1,)), ((), ())),
                              preferred_element_type=F32)
        ms = jnp.mean(cqt * cqt, axis=0, keepdims=True)
        cqt_s[...] = (cqt * lax.rsqrt(ms + EPS) * lat_ref[...]).astype(BF16)

    rows = hg * (nope + rope)
    wut = wut_ref[pl.ds(pl.multiple_of(gidx * rows, rows), rows), :]
    qt = jnp.dot(wut, cqt_s[...], preferred_element_type=F32) * q_scale
    cos_t = cos_ref[...]
    sin_t = sin_ref[...]
    hd = nope + rope
    half = rope // 2
    pad = qt_ref.shape[2] - hd
    for hh in range(hg):
        b = hh * hd
        qt_ref[hh, 0, 0:nope] = qt[b:b + nope].astype(BF16)
        x1 = qt[b + nope:b + nope + half]
        x2 = qt[b + nope + half:b + hd]
        qt_ref[hh, 0, nope:nope + half] = (x1 * cos_t - x2 * sin_t).astype(BF16)
        qt_ref[hh, 0, nope + half:hd] = (x2 * cos_t + x1 * sin_t).astype(BF16)
        if pad:
            qt_ref[hh, 0, hd:hd + pad] = jnp.zeros((pad, qt.shape[1]), BF16)


def _q_proj(h, norm_g, wdqt_bf16, lat_g, cos_t, sin_t, wuqt_bf16, *, n_heads, nope, rope, hd_pad,
            q_scale, tile, tm=512, hg=16):
    s_len, d = h.shape
    qlora = wdqt_bf16.shape[0]
    hd = nope + rope
    hg = min(hg, n_heads)
    tm = min(tm, tile)
    per = tile // tm
    assert s_len % tile == 0 and tile % tm == 0 and n_heads % hg == 0
    assert wuqt_bf16.shape == (n_heads * hd, qlora)
    return pl.pallas_call(
        functools.partial(_q_kernel, nope=nope, rope=rope, hg=hg, q_scale=q_scale),
        grid=(s_len // tm, n_heads // hg),
        in_specs=[
            pl.BlockSpec((tm, d), lambda i, g: (i, 0)),
            _const_spec((1, d)),
            _const_spec((qlora, d)),
            _const_spec((qlora, 1)),
            pl.BlockSpec((rope // 2, tm), lambda i, g: (0, i)),
            pl.BlockSpec((rope // 2, tm), lambda i, g: (0, i)),
            _const_spec((n_heads * hd, qlora)),
        ],
        out_specs=pl.BlockSpec((hg, 1, hd_pad, tm), lambda i, g: (g, i // per, 0, i % per)),
        out_shape=jax.ShapeDtypeStruct((n_heads, s_len // tile, hd_pad, tile), BF16),
        scratch_shapes=[pltpu.VMEM((qlora, tm), BF16)],
        compiler_params=_params("parallel", "arbitrary"),
        name="q_proj",
    )(h, norm_g.reshape(1, d), wdqt_bf16, lat_g.reshape(qlora, 1), cos_t, sin_t, wuqt_bf16)


def _attn_kernel(qt_ref, k_ref, vt_ref, *rest, tq, tk, cb, vd, n_cast):
    w_f32 = rest[:n_cast]
    o_ref = rest[n_cast]
    w_bf16 = rest[n_cast + 1:2 * n_cast + 1]
    s0_ref, s1_ref, cmax0_ref, cmax1_ref, acc_ref, m_ref = rest[2 * n_cast + 1:]
    cmax_buf = (cmax0_ref, cmax1_ref)
    for src, dst in zip(w_f32, w_bf16):
        dst[...] = src[0].astype(BF16)

    i = pl.program_id(1)
    tile = qt_ref.shape[3]
    r = tq // tk
    s_buf = (s0_ref, s1_ref)

    def scores(j, slot, c0, c1, rows=tk):
        kblk = k_ref[0, pl.ds(pl.multiple_of(j * tk, tk), rows), :]
        qt = qt_ref[0, c0 // tile, :, c0 % tile:c0 % tile + (c1 - c0)]
        s = jnp.dot(kblk, qt, preferred_element_type=F32)
        s_buf[slot][0:rows, c0:c1] = s
        cmax_buf[slot][:, c0:c1] = jnp.max(s, axis=0, keepdims=True)

    def visible_rows(c1, key_offset):
        return min(tk, c1 - key_offset)

    def weights_values(j, slot, c0, c1, key_offset):
        rows = tk if key_offset is None else visible_rows(c1, key_offset)
        s = s_buf[slot][0:rows, c0:c1]
        if key_offset is not None and key_offset + rows - 1 > c0:
            kpos = lax.broadcasted_iota(jnp.int32, s.shape, 0) + key_offset
            qpos = lax.broadcasted_iota(jnp.int32, s.shape, 1) + c0
            s = jnp.where(kpos <= qpos, s, MASK_VALUE)
            cmax = jnp.max(s, axis=0, keepdims=True)
        else:
            cmax = cmax_buf[slot][:, c0:c1]
        m = m_ref[:, c0:c1]
        m_new = jnp.maximum(m, cmax)
        alpha = jnp.exp2(m - m_new)
        p = jnp.exp2(s - m_new)
        m_ref[:, c0:c1] = m_new
        vt = vt_ref[0, j, :, 0:rows]
        acc_ref[:, c0:c1] = alpha * acc_ref[:, c0:c1] + jnp.dot(vt, p.astype(BF16),
                                                               preferred_element_type=F32)

    def col_blocks(lo):
        return [(c0, min(c0 + cb, tq)) for c0 in range(lo, tq, cb)]

    def stage(j, slot, lo=0, diagonal=False, next_lo=0):
        cur = col_blocks(lo)
        nxt = col_blocks(next_lo) if next_lo is not None else []
        for idx in range(max(len(cur), len(nxt))):
            if idx < len(nxt):
                scores(j + 1, 1 - slot, *nxt[idx],
                       rows=visible_rows(nxt[idx][1], next_lo) if diagonal else tk)
            if idx < len(cur):
                weights_values(j, slot, *cur[idx], key_offset=lo if diagonal else None)

    m_ref[...] = jnp.full(m_ref.shape, MASK_VALUE, F32)
    acc_ref[...] = jnp.zeros(acc_ref.shape, F32)
    for c0, c1 in col_blocks(0):
        scores(0, 0, c0, c1)

    def group(g, carry):
        for c in range(r):
            stage(g * r + c, c % 2)
        return carry

    lax.fori_loop(0, i, group, 0)
    for c in range(r):
        stage(i * r + c, c % 2, lo=c * tk, diagonal=True, next_lo=(c + 1) * tk if c < r - 1 else None)
    o_ref[...] = (acc_ref[0:vd] / acc_ref[vd:vd + 1]).T.astype(o_ref.dtype)


def _cast_block_rows(n_rows, n_steps):
    for rb in range(BF16_SUBLANES, n_rows + 1, BF16_SUBLANES):
        if n_rows % rb == 0 and n_rows // rb <= n_steps:
            return rb
    return None


def _attention(qt, k, vt, stacked_weights, *, vd, tq=2048, tk=512, cb=512):
    n_heads, n_tiles, hd, tile = qt.shape
    s_len = n_tiles * tile
    vda = vt.shape[2]
    tq = min(tq, s_len)
    cb = min(cb, tq)
    assert k.shape == (n_heads, s_len, hd) and vt.shape == (n_heads, n_tiles, vda, tile) and vda > vd
    assert s_len % tq == 0 and tq % (2 * tk) == 0 and tk == tile and tile % cb == 0
    nq = s_len // tq
    cast_in, cast_out, cast_shapes = [], [], []
    for w, layer in stacked_weights:
        _, n_rows, n_cols = w.shape
        rb = _cast_block_rows(n_rows, n_heads * nq)
        assert rb is not None
        blk = lambda h, i, last=n_rows // rb - 1: jnp.minimum(h * nq + i, last)
        cast_in.append(pl.BlockSpec((1, rb, n_cols), lambda h, i, blk=blk, layer=layer: (layer, blk(h, i), 0)))
        cast_out.append(pl.BlockSpec((rb, n_cols), lambda h, i, blk=blk: (blk(h, i), 0)))
        cast_shapes.append(jax.ShapeDtypeStruct((n_rows, n_cols), BF16))
    outs = pl.pallas_call(
        functools.partial(_attn_kernel, tq=tq, tk=tk, cb=cb, vd=vd, n_cast=len(stacked_weights)),
        grid=(n_heads, nq),
        in_specs=[
            pl.BlockSpec((1, tq // tile, hd, tile), lambda h, i: (h, i, 0, 0)),
            pl.BlockSpec((1, s_len, hd), lambda h, i: (h, 0, 0)),
            pl.BlockSpec((1, n_tiles, vda, tile), lambda h, i: (h, 0, 0, 0)),
        ] + cast_in,
        out_specs=[pl.BlockSpec((tq, vd), lambda h, i: (i, h))] + cast_out,
        out_shape=[jax.ShapeDtypeStruct((s_len, n_heads * vd), BF16)] + cast_shapes,
        scratch_shapes=[
            pltpu.VMEM((tk, tq), F32),
            pltpu.VMEM((tk, tq), F32),
            pltpu.VMEM((1, tq), F32),
            pltpu.VMEM((1, tq), F32),
            pltpu.VMEM((vda, tq), F32),
            pltpu.VMEM((1, tq), F32),
        ],
        compiler_params=_params("arbitrary", "arbitrary"),
        name="mla_attention",
    )(qt, k, vt, *[w for w, _ in stacked_weights])
    return outs[0], outs[1:]


def _oproj_kernel(a_ref, w_ref, r_ref, o_ref):
    o_ref[...] = r_ref[...] + jnp.dot(a_ref[...], w_ref[...], preferred_element_type=F32)


def _out_proj(a_bf16, w_bf16, resid, *, tm=512, tn=1024):
    s_len, kdim = a_bf16.shape
    d = w_bf16.shape[1]
    tn = min(tn, d)
    assert s_len % tm == 0 and d % tn == 0
    return pl.pallas_call(
        _oproj_kernel,
        grid=(s_len // tm, d // tn),
        in_specs=[
            pl.BlockSpec((tm, kdim), lambda i, j: (i, 0)),
            pl.BlockSpec((kdim, tn), lambda i, j: (0, j)),
            pl.BlockSpec((tm, tn), lambda i, j: (i, j)),
        ],
        out_specs=pl.BlockSpec((tm, tn), lambda i, j: (i, j)),
        out_shape=jax.ShapeDtypeStruct((s_len, d), F32),
        compiler_params=_params("parallel", "parallel"),
        name="attn_out_proj",
    )(a_bf16, w_bf16, resid)


def _trunk(x, a_norm, a_pool_w, a_scale, kv_norm, w_dkv, kv_lat_norm, w_ukv, b_norm, w_dq, q_lat_norm,
           w_uq, w_o, ffn_norm, ffn_w_in, ffn_conv_w, ffn_conv_b, ffn_w_out, final_norm,
           *, pool_ts=256, ffn_tm=512, ffn_tf=256, q_tm=512, kv_tm=256, attn_tq=2048, attn_tk=512, attn_cb=256,
           oproj_tm=512):
    batch, s_len, d = x.shape
    depth = ffn_norm.shape[0]
    n_a = a_norm.shape[0]
    n_b = b_norm.shape[0]
    assert n_a + n_b == depth and n_a >= 1
    lora, n_heads, kvd = w_ukv.shape
    hd = w_uq.shape[-1]
    rope = w_dkv.shape[1] - lora
    nope = hd - rope
    vd = kvd - nope
    half = rope // 2
    hd_pad = -(-hd // LANES) * LANES

    inv_freq = ROPE_THETA ** (-jnp.arange(0, rope, 2, dtype=F32) / rope)
    ang = jnp.arange(s_len, dtype=F32)[:, None] * inv_freq[None, :]
    cos, sin = jnp.cos(ang), jnp.sin(ang)
    cos2 = jnp.concatenate([cos, cos], axis=1)
    sin2 = jnp.concatenate([-sin, sin], axis=1)
    cos_t, sin_t = cos.T, sin.T

    bf = lambda w: w.astype(BF16)
    wd_ext = bf(jnp.concatenate([w_dkv, w_dkv[:, lora + half:], w_dkv[:, lora:lora + half]], axis=1))
    wuk = bf(w_ukv[:, :, :nope].reshape(lora, n_heads * nope))
    wuvt = bf(w_ukv[:, :, nope:].reshape(lora, n_heads * vd).T)
    final_g = final_norm

    outs = []
    ffn_bf16 = {}
    wo_bf16 = {}
    for b in range(batch):
        h = x.reshape(s_len, d) if batch == 1 else x[b]
        shared = None
        for layer in range(depth):
            last = layer == depth - 1
            if layer < n_a:
                h = _pool_layer(h, a_norm[layer], bf(a_pool_w[layer]), a_scale[layer], ts=pool_ts)
            else:
                j = layer - n_a
                qlora = w_dq.shape[2]
                qt = _q_proj(h, b_norm[j], bf(w_dq[j].T), q_lat_norm[j], cos_t, sin_t,
                             bf(w_uq[j].reshape(qlora, n_heads * hd).T),
                             n_heads=n_heads, nope=nope, rope=rope, hd_pad=hd_pad,
                             q_scale=hd ** -0.5 * math.log2(math.e), tile=attn_tk, tm=q_tm)
                first = layer not in ffn_bf16
                to_cast = ((ffn_w_in, layer), (ffn_w_out, layer), (w_o, j)) if first else ()
                o, cast = _attention(qt, shared[0], shared[1], to_cast, vd=vd, tq=attn_tq, tk=attn_tk,
                                     cb=attn_cb)
                if first:
                    ffn_bf16[layer] = cast[:2]
                    wo_bf16[j] = cast[2]
                h = _out_proj(o, wo_bf16[j], h, tm=oproj_tm)
            if layer not in ffn_bf16:
                ffn_bf16[layer] = (bf(ffn_w_in[layer]), bf(ffn_w_out[layer]))
            h = _ffn_layer(h, ffn_norm[layer], ffn_bf16[layer][0], ffn_conv_w[layer], ffn_conv_b[layer],
                           ffn_bf16[layer][1], final_g, final=last, tm=ffn_tm, tf=ffn_tf)
            if layer == n_a - 1:
                shared = _kv_proj(h, kv_norm, wd_ext, kv_lat_norm, cos2, sin2, wuk, wuvt,
                                  n_heads=n_heads, nope=nope, rope=rope, hd_pad=hd_pad,
                                  ones_rows=BF16_SUBLANES, tile=attn_tk, tm=kv_tm)
        outs.append(h)
    return outs[0].reshape(1, s_len, d) if batch == 1 else jnp.stack(outs, axis=0)


def kernel(x, a_norm, a_pool_w, a_scale, kv_norm, w_dkv, kv_lat_norm, w_ukv, b_norm, w_dq, q_lat_norm,
           w_uq, w_o, ffn_norm, ffn_w_in, ffn_conv_w, ffn_conv_b, ffn_w_out, final_norm):
    return _trunk(x, a_norm, a_pool_w, a_scale, kv_norm, w_dkv, kv_lat_norm, w_ukv, b_norm, w_dq,
                  q_lat_norm, w_uq, w_o, ffn_norm, ffn_w_in, ffn_conv_w, ffn_conv_b, ffn_w_out,
                  final_norm)
```

```python
import functools
import math

import jax
import jax.numpy as jnp
from jax import lax
from jax.experimental import pallas as pl
from jax.experimental.pallas import tpu as pltpu

EPS = 1e-6
ROPE_THETA = 10000.0
POOL_WINDOWS = (2, 4, 8, 16)
CONV_WIDTH = 3
POOL_HALO = 16
CONV_HALO = 8
NORM_PIECES = 4
CONV_PIECES = 16
MASK_VALUE = -1e30
VMEM_LIMIT_BYTES = 56 * 1024 * 1024
LANES = 128
BF16_SUBLANES = 16

F32 = jnp.float32
BF16 = jnp.bfloat16


def _rms(x, g):
    return x * lax.rsqrt(jnp.mean(x * x, axis=-1, keepdims=True) + EPS) * g


def _params(*sem):
    return pltpu.CompilerParams(dimension_semantics=sem, vmem_limit_bytes=VMEM_LIMIT_BYTES)


def _const_spec(shape):
    nd = len(shape)
    return pl.BlockSpec(shape, lambda *_: (0,) * nd, pipeline_mode=pl.Buffered(1))


def _pool_kernel(x_ref, halo_ref, g_ref, w_ref, sc_ref, o_ref, *, ts, gc):
    i = pl.program_id(0)
    g = g_ref[...]
    x = x_ref[...]
    xn = _rms(x, g)
    hn = jnp.where(i == 0, 0.0, _rms(halo_ref[...], g))
    pos = i * ts + lax.broadcasted_iota(jnp.int32, (ts, 1), 0)
    for gi, w in enumerate(POOL_WINDOWS):
        sl = slice(gi * gc, (gi + 1) * gc)
        xg = xn[:, sl]
        s = jnp.concatenate([hn[:, sl], xg], axis=0)
        k = 1
        while k < w:
            s = s + pltpu.roll(s, k, axis=0)
            k *= 2
        cnt = jnp.minimum(pos + 1, w).astype(F32)
        pooled = s[POOL_HALO:] / cnt - xg
        mixed = jnp.dot(pooled.astype(BF16), w_ref[gi], preferred_element_type=F32)
        o_ref[:, sl] = x[:, sl] + mixed * sc_ref[:, sl]


def _pool_layer(x, norm_g, w_bf16, scale, *, ts=256):
    s_len, d = x.shape
    ng, gc, _ = w_bf16.shape
    assert ng == len(POOL_WINDOWS) and ng * gc == d and s_len % ts == 0 and ts % POOL_HALO == 0
    per = ts // POOL_HALO
    return pl.pallas_call(
        functools.partial(_pool_kernel, ts=ts, gc=gc),
        grid=(s_len // ts,),
        in_specs=[
            pl.BlockSpec((ts, d), lambda i: (i, 0)),
            pl.BlockSpec((POOL_HALO, d), lambda i: (jnp.maximum(i * per - 1, 0), 0)),
            _const_spec((1, d)),
            _const_spec((ng, gc, gc)),
            _const_spec((1, d)),
        ],
        out_specs=pl.BlockSpec((ts, d), lambda i: (i, 0)),
        out_shape=jax.ShapeDtypeStruct((s_len, d), F32),
        compiler_params=_params("parallel"),
        name="pool_layer",
    )(x, x, norm_g.reshape(1, d), w_bf16, scale.reshape(1, d))


def _ffn_kernel(h_ref, g_ref, wg_ref, wv_ref, cwg_ref, cwv_ref, bg_ref, bv_ref, wo_ref, fg_ref,
                o_ref, xn_s, a0_s, a1_s, ug0_s, ug1_s, uv0_s, uv1_s, cg_s, cv_s, *, tm, tn, nf, final):
    i = pl.program_id(0)
    f = pl.program_id(1)
    acts = (a0_s, a1_s)
    ugs = (ug0_s, ug1_s)
    uvs = (uv0_s, uv1_s)

    c = jnp.clip(f - 1, 0, nf - 1)

    @pl.when((i == 0) & (f >= 1) & (f <= nf))
    def _():
        cg_s[c] = jnp.zeros(cg_s.shape[1:], F32)
        cv_s[c] = jnp.zeros(cv_s.shape[1:], F32)

    n_down = o_ref.shape[1] // tn
    rows = tm // CONV_PIECES

    def up_piece(p, which, half):
        u_s, w_ref = ((ugs[p], wg_ref), (uvs[p], wv_ref))[which]
        r = slice(half * (tm // 2), (half + 1) * (tm // 2))
        u_s[r] = jnp.dot(xn_s[r], w_ref[...], preferred_element_type=F32)

    def conv_rows(u_s, cw_ref, b_ref, c_s, r0):
        prev = c_s[c] if r0 == 0 else u_s[r0 - CONV_HALO:r0]
        u = u_s[r0:r0 + rows]
        cat = jnp.concatenate([prev, u], axis=0)
        u1 = pltpu.roll(cat, 1, axis=0)[CONV_HALO:]
        u2 = pltpu.roll(cat, 2, axis=0)[CONV_HALO:]
        cw = cw_ref[...]
        return cw[0:1] * u2 + cw[1:2] * u1 + cw[2:3] * u + b_ref[...]

    def conv_piece(p, k):
        r0 = k * rows
        gate = conv_rows(ugs[p], cwg_ref, bg_ref, cg_s, r0)
        val = conv_rows(uvs[p], cwv_ref, bv_ref, cv_s, r0)
        acts[p][r0:r0 + rows] = (gate * (1.0 / (1.0 + jnp.exp(-gate))) * val).astype(BF16)
        if k == CONV_PIECES - 1:
            cg_s[c] = ugs[p][tm - CONV_HALO:]
            cv_s[c] = uvs[p][tm - CONV_HALO:]

    def down_piece(p, n):
        sl = slice(n * tn, (n + 1) * tn)
        wo = wo_ref[:, sl] if wo_ref.ndim == 2 else wo_ref[0, :, sl].astype(BF16)
        o_ref[:, sl] += jnp.dot(acts[p][...], wo, preferred_element_type=F32)

    def run(p_up=None, p_conv=None, p_down=None):
        mxu, vpu = [], []
        if p_up is not None:
            mxu += [functools.partial(up_piece, p_up, w, half) for half in range(2) for w in range(2)]
        if p_down is not None:
            mxu += [functools.partial(down_piece, p_down, n) for n in range(n_down)]
        if p_conv is not None:
            vpu += [functools.partial(conv_piece, p_conv, k) for k in range(CONV_PIECES)]
        for idx in range(max(len(mxu), len(vpu))):
            if idx < len(mxu):
                mxu[idx]()
            if idx < len(vpu):
                vpu[idx]()

    @pl.when(f == 0)
    def _():
        rn = tm // NORM_PIECES

        def norm_rows(k):
            r = slice(k * rn, (k + 1) * rn)
            h = h_ref[r]
            xn_s[r] = _rms(h, g_ref[...]).astype(BF16)
            o_ref[r] = h

        def up_rows(k):
            r = slice(k * rn, (k + 1) * rn)
            ugs[0][r] = jnp.dot(xn_s[r], wg_ref[...], preferred_element_type=F32)
            uvs[0][r] = jnp.dot(xn_s[r], wv_ref[...], preferred_element_type=F32)

        norm_rows(0)
        for k in range(NORM_PIECES):
            if k + 1 < NORM_PIECES:
                norm_rows(k + 1)
            up_rows(k)

    @pl.when(f == 1)
    def _():
        run(p_up=1, p_conv=0)

    for p in range(2):
        @pl.when((f >= 2) & (f < nf) & (f % 2 == p))
        def _():
            run(p_up=p, p_conv=1 - p, p_down=p)

    @pl.when(f == nf)
    def _():
        run(p_conv=(nf - 1) % 2, p_down=nf % 2)

    @pl.when(f == nf + 1)
    def _():
        run(p_down=(nf - 1) % 2)
        if final:
            o_ref[...] = _rms(o_ref[...], fg_ref[...])


def _ffn_layer(h, norm_g, w_in_bf16, conv_w, conv_b, w_out, final_g, *, final, w_out_layer=None, tm=512,
               tf=256, tn=256):
    s_len, d = h.shape
    dff = w_out.shape[-2]
    tn = min(tn, d)
    assert s_len % tm == 0 and dff % tf == 0 and d % tn == 0 and w_in_bf16.shape == (d, 2 * dff)
    nf = dff // tf
    assert nf >= 2
    conv_b2 = conv_b.reshape(1, 2 * dff)
    up = lambda f: jnp.minimum(f, nf - 1)
    cv = lambda f: jnp.clip(f - 1, 0, nf - 1)
    dn = lambda f: jnp.clip(f - 2, 0, nf - 1)
    return pl.pallas_call(
        functools.partial(_ffn_kernel, tm=tm, tn=tn, nf=nf, final=final),
        grid=(s_len // tm, nf + 2),
        in_specs=[
            pl.BlockSpec((tm, d), lambda i, f: (i, 0), pipeline_mode=pl.Buffered(1)),
            _const_spec((1, d)),
            pl.BlockSpec((d, tf), lambda i, f: (0, up(f))),
            pl.BlockSpec((d, tf), lambda i, f: (0, nf + up(f))),
            pl.BlockSpec((CONV_WIDTH, tf), lambda i, f: (0, cv(f))),
            pl.BlockSpec((CONV_WIDTH, tf), lambda i, f: (0, nf + cv(f))),
            pl.BlockSpec((1, tf), lambda i, f: (0, cv(f))),
            pl.BlockSpec((1, tf), lambda i, f: (0, nf + cv(f))),
            (pl.BlockSpec((tf, d), lambda i, f: (dn(f), 0)) if w_out_layer is None else
             pl.BlockSpec((1, tf, d), lambda i, f: (w_out_layer, dn(f), 0))),
            _const_spec((1, d)),
        ],
        out_specs=pl.BlockSpec((tm, d), lambda i, f: (i, 0)),
        out_shape=jax.ShapeDtypeStruct((s_len, d), F32),
        scratch_shapes=[
            pltpu.VMEM((tm, d), BF16),
            pltpu.VMEM((tm, tf), BF16),
            pltpu.VMEM((tm, tf), BF16),
            pltpu.VMEM((tm, tf), F32),
            pltpu.VMEM((tm, tf), F32),
            pltpu.VMEM((tm, tf), F32),
            pltpu.VMEM((tm, tf), F32),
            pltpu.VMEM((nf, CONV_HALO, tf), F32),
            pltpu.VMEM((nf, CONV_HALO, tf), F32),
        ],
        compiler_params=_params("arbitrary", "arbitrary"),
        name="conv_ffn",
    )(h, norm_g.reshape(1, d), w_in_bf16, w_in_bf16, conv_w, conv_w, conv_b2, conv_b2, w_out,
      final_g.reshape(1, d))


def _kv_kernel(h_ref, g_ref, wd_ref, lat_ref, cos_ref, sin_ref, wuk_ref, wuvt_ref,
               k_ref, vt_ref, c_s, ct_s, kr_s, *, lora, rope, nope, vd, hg):
    gidx = pl.program_id(1)

    @pl.when(gidx == 0)
    def _():
        hn = _rms(h_ref[...], g_ref[...]).astype(BF16)
        ckv = jnp.dot(hn, wd_ref[...], preferred_element_type=F32)
        c = _rms(ckv[:, :lora], lat_ref[...])
        c_s[...] = c.astype(BF16)
        ct_s[...] = c.T.astype(BF16)
        kr = ckv[:, lora:lora + rope] * cos_ref[...] + ckv[:, lora + rope:] * sin_ref[...]
        kr_s[...] = kr.astype(BF16)

    wuk = wuk_ref[:, pl.ds(pl.multiple_of(gidx * (hg * nope), hg * nope), hg * nope)]
    wuvt = wuvt_ref[pl.ds(pl.multiple_of(gidx * (hg * vd), hg * vd), hg * vd), :]
    kn = jnp.dot(c_s[...], wuk, preferred_element_type=F32)
    vt = jnp.dot(wuvt, ct_s[...], preferred_element_type=F32)
    tm = kn.shape[0]
    hd = nope + rope
    k_pad = k_ref.shape[2] - hd
    ones_rows = vt_ref.shape[2] - vd
    for hh in range(hg):
        k_ref[hh, :, 0:nope] = kn[:, hh * nope:(hh + 1) * nope].astype(BF16)
        k_ref[hh, :, nope:hd] = kr_s[...]
        if k_pad:
            k_ref[hh, :, hd:hd + k_pad] = jnp.zeros((tm, k_pad), BF16)
        vt_ref[hh, 0, 0:vd] = vt[hh * vd:(hh + 1) * vd].astype(BF16)
        vt_ref[hh, 0, vd:vd + ones_rows] = jnp.ones((ones_rows, tm), BF16)


def _kv_proj(h, norm_g, wd_ext_bf16, lat_g, cos2, sin2, wuk_bf16, wuvt_bf16, *, n_heads, nope, rope,
             hd_pad, ones_rows, tile, tm=256, hg=32):
    s_len, d = h.shape
    lora = wuk_bf16.shape[0]
    vd = wuvt_bf16.shape[0] // n_heads
    hg = min(hg, n_heads)
    tm = min(tm, tile)
    per = tile // tm
    assert s_len % tile == 0 and tile % tm == 0 and n_heads % hg == 0
    assert wd_ext_bf16.shape == (d, lora + 2 * rope)
    return pl.pallas_call(
        functools.partial(_kv_kernel, lora=lora, rope=rope, nope=nope, vd=vd, hg=hg),
        grid=(s_len // tm, n_heads // hg),
        in_specs=[
            pl.BlockSpec((tm, d), lambda i, g: (i, 0)),
            _const_spec((1, d)),
            _const_spec((d, lora + 2 * rope)),
            _const_spec((1, lora)),
            pl.BlockSpec((tm, rope), lambda i, g: (i, 0)),
            pl.BlockSpec((tm, rope), lambda i, g: (i, 0)),
            _const_spec((lora, n_heads * nope)),
            _const_spec((n_heads * vd, lora)),
        ],
        out_specs=[
            pl.BlockSpec((hg, tm, hd_pad), lambda i, g: (g, i, 0)),
            pl.BlockSpec((hg, 1, vd + ones_rows, tm), lambda i, g: (g, i // per, 0, i % per)),
        ],
        out_shape=[
            jax.ShapeDtypeStruct((n_heads, s_len, hd_pad), BF16),
            jax.ShapeDtypeStruct((n_heads, s_len // tile, vd + ones_rows, tile), BF16),
        ],
        scratch_shapes=[
            pltpu.VMEM((tm, lora), BF16),
            pltpu.VMEM((lora, tm), BF16),
            pltpu.VMEM((tm, rope), BF16),
        ],
        compiler_params=_params("parallel", "arbitrary"),
        name="kv_proj",
    )(h, norm_g.reshape(1, d), wd_ext_bf16, lat_g.reshape(1, lora), cos2, sin2, wuk_bf16, wuvt_bf16)


def _q_kernel(h_ref, g_ref, wdt_ref, lat_ref, cos_ref, sin_ref, wut_ref, qt_ref, cqt_s,
              *, nope, rope, hg, q_scale):
    gidx = pl.program_id(1)

    @pl.when(gidx == 0)
    def _():
        xn = _rms(h_ref[...], g_ref[...]).astype(BF16)
        cqt = lax.dot_general(wdt_ref[...], xn, (((1,), (1,)), ((), ())),
                              preferred_element_type=F32)
        ms = jnp.mean(cqt * cqt, axis=0, keepdims=True)
        cqt_s[...] = (cqt * lax.rsqrt(ms + EPS) * lat_ref[...]).astype(BF16)

    rows = hg * (nope + rope)
    wut = wut_ref[pl.ds(pl.multiple_of(gidx * rows, rows), rows), :]
    qt = jnp.dot(wut, cqt_s[...], preferred_element_type=F32) * q_scale
    cos_t = cos_ref[...]
    sin_t = sin_ref[...]
    hd = nope + rope
    half = rope // 2
    pad = qt_ref.shape[2] - hd
    for hh in range(hg):
        b = hh * hd
        qt_ref[hh, 0, 0:nope] = qt[b:b + nope].astype(BF16)
        x1 = qt[b + nope:b + nope + half]
        x2 = qt[b + nope + half:b + hd]
        qt_ref[hh, 0, nope:nope + half] = (x1 * cos_t - x2 * sin_t).astype(BF16)
        qt_ref[hh, 0, nope + half:hd] = (x2 * cos_t + x1 * sin_t).astype(BF16)
        if pad:
            qt_ref[hh, 0, hd:hd + pad] = jnp.zeros((pad, qt.shape[1]), BF16)


def _q_proj(h, norm_g, wdqt_bf16, lat_g, cos_t, sin_t, wuqt_bf16, *, n_heads, nope, rope, hd_pad,
            q_scale, tile, tm=512, hg=16):
    s_len, d = h.shape
    qlora = wdqt_bf16.shape[0]
    hd = nope + rope
    hg = min(hg, n_heads)
    tm = min(tm, tile)
    per = tile // tm
    assert s_len % tile == 0 and tile % tm == 0 and n_heads % hg == 0
    assert wuqt_bf16.shape == (n_heads * hd, qlora)
    return pl.pallas_call(
        functools.partial(_q_kernel, nope=nope, rope=rope, hg=hg, q_scale=q_scale),
        grid=(s_len // tm, n_heads // hg),
        in_specs=[
            pl.BlockSpec((tm, d), lambda i, g: (i, 0)),
            _const_spec((1, d)),
            _const_spec((qlora, d)),
            _const_spec((qlora, 1)),
            pl.BlockSpec((rope // 2, tm), lambda i, g: (0, i)),
            pl.BlockSpec((rope // 2, tm), lambda i, g: (0, i)),
            _const_spec((n_heads * hd, qlora)),
        ],
        out_specs=pl.BlockSpec((hg, 1, hd_pad, tm), lambda i, g: (g, i // per, 0, i % per)),
        out_shape=jax.ShapeDtypeStruct((n_heads, s_len // tile, hd_pad, tile), BF16),
        scratch_shapes=[pltpu.VMEM((qlora, tm), BF16)],
        compiler_params=_params("parallel", "arbitrary"),
        name="q_proj",
    )(h, norm_g.reshape(1, d), wdqt_bf16, lat_g.reshape(qlora, 1), cos_t, sin_t, wuqt_bf16)


def _attn_kernel(qt_ref, k_ref, vt_ref, *rest, tq, tk, cb, vd, n_cast):
    w_f32 = rest[:n_cast]
    o_ref = rest[n_cast]
    w_bf16 = rest[n_cast + 1:2 * n_cast + 1]
    s0_ref, s1_ref, cmax0_ref, cmax1_ref, acc_ref, m_ref = rest[2 * n_cast + 1:]
    cmax_buf = (cmax0_ref, cmax1_ref)
    for src, dst in zip(w_f32, w_bf16):
        dst[...] = src[0].astype(BF16)

    i = pl.program_id(1)
    tile = qt_ref.shape[3]
    r = tq // tk
    s_buf = (s0_ref, s1_ref)

    def scores(j, slot, c0, c1, rows=tk):
        kblk = k_ref[0, pl.ds(pl.multiple_of(j * tk, tk), rows), :]
        qt = qt_ref[0, c0 // tile, :, c0 % tile:c0 % tile + (c1 - c0)]
        s = jnp.dot(kblk, qt, preferred_element_type=F32)
        s_buf[slot][0:rows, c0:c1] = s
        cmax_buf[slot][:, c0:c1] = jnp.max(s, axis=0, keepdims=True)

    def visible_rows(c1, key_offset):
        return min(tk, c1 - key_offset)

    def weights_values(j, slot, c0, c1, key_offset):
        rows = tk if key_offset is None else visible_rows(c1, key_offset)
        s = s_buf[slot][0:rows, c0:c1]
        if key_offset is not None and key_offset + rows - 1 > c0:
            kpos = lax.broadcasted_iota(jnp.int32, s.shape, 0) + key_offset
            qpos = lax.broadcasted_iota(jnp.int32, s.shape, 1) + c0
            s = jnp.where(kpos <= qpos, s, MASK_VALUE)
            cmax = jnp.max(s, axis=0, keepdims=True)
        else:
            cmax = cmax_buf[slot][:, c0:c1]
        m = m_ref[:, c0:c1]
        m_new = jnp.maximum(m, cmax)
        alpha = jnp.exp2(m - m_new)
        p = jnp.exp2(s - m_new)
        m_ref[:, c0:c1] = m_new
        vt = vt_ref[0, j, :, 0:rows]
        acc_ref[:, c0:c1] = alpha * acc_ref[:, c0:c1] + jnp.dot(vt, p.astype(BF16),
                                                               preferred_element_type=F32)

    def col_blocks(lo):
        return [(c0, min(c0 + cb, tq)) for c0 in range(lo, tq, cb)]

    def stage(j, slot, lo=0, diagonal=False, next_lo=0):
        cur = col_blocks(lo)
        nxt = col_blocks(next_lo) if next_lo is not None else []
        for idx in range(max(len(cur), len(nxt))):
            if idx < len(nxt):
                scores(j + 1, 1 - slot, *nxt[idx],
                       rows=visible_rows(nxt[idx][1], next_lo) if diagonal else tk)
            if idx < len(cur):
                weights_values(j, slot, *cur[idx], key_offset=lo if diagonal else None)

    m_ref[...] = jnp.full(m_ref.shape, MASK_VALUE, F32)
    acc_ref[...] = jnp.zeros(acc_ref.shape, F32)
    for c0, c1 in col_blocks(0):
        scores(0, 0, c0, c1)

    def group(g, carry):
        for c in range(r):
            stage(g * r + c, c % 2)
        return carry

    lax.fori_loop(0, i, group, 0)
    for c in range(r):
        stage(i * r + c, c % 2, lo=c * tk, diagonal=True, next_lo=(c + 1) * tk if c < r - 1 else None)
    o_ref[...] = (acc_ref[0:vd] / acc_ref[vd:vd + 1]).T.astype(o_ref.dtype)


def _cast_block_rows(n_rows, n_steps):
    for rb in range(BF16_SUBLANES, n_rows + 1, BF16_SUBLANES):
        if n_rows % rb == 0 and n_rows // rb <= n_steps:
            return rb
    return None


def _attention(qt, k, vt, stacked_weights, *, vd, tq=2048, tk=512, cb=512):
    n_heads, n_tiles, hd, tile = qt.shape
    s_len = n_tiles * tile
    vda = vt.shape[2]
    tq = min(tq, s_len)
    cb = min(cb, tq)
    assert k.shape == (n_heads, s_len, hd) and vt.shape == (n_heads, n_tiles, vda, tile) and vda > vd
    assert s_len % tq == 0 and tq % (2 * tk) == 0 and tk == tile and tile % cb == 0
    nq = s_len // tq
    cast_in, cast_out, cast_shapes = [], [], []
    for w, layer in stacked_weights:
        _, n_rows, n_cols = w.shape
        rb = _cast_block_rows(n_rows, n_heads * nq)
        assert rb is not None
        blk = lambda h, i, last=n_rows // rb - 1: jnp.minimum(h * nq + i, last)
        cast_in.append(pl.BlockSpec((1, rb, n_cols), lambda h, i, blk=blk, layer=layer: (layer, blk(h, i), 0)))
        cast_out.append(pl.BlockSpec((rb, n_cols), lambda h, i, blk=blk: (blk(h, i), 0)))
        cast_shapes.append(jax.ShapeDtypeStruct((n_rows, n_cols), BF16))
    outs = pl.pallas_call(
        functools.partial(_attn_kernel, tq=tq, tk=tk, cb=cb, vd=vd, n_cast=len(stacked_weights)),
        grid=(n_heads, nq),
        in_specs=[
            pl.BlockSpec((1, tq // tile, hd, tile), lambda h, i: (h, i, 0, 0)),
            pl.BlockSpec((1, s_len, hd), lambda h, i: (h, 0, 0)),
            pl.BlockSpec((1, n_tiles, vda, tile), lambda h, i: (h, 0, 0, 0)),
        ] + cast_in,
        out_specs=[pl.BlockSpec((tq, vd), lambda h, i: (i, h))] + cast_out,
        out_shape=[jax.ShapeDtypeStruct((s_len, n_heads * vd), BF16)] + cast_shapes,
        scratch_shapes=[
            pltpu.VMEM((tk, tq), F32),
            pltpu.VMEM((tk, tq), F32),
            pltpu.VMEM((1, tq), F32),
            pltpu.VMEM((1, tq), F32),
            pltpu.VMEM((vda, tq), F32),
            pltpu.VMEM((1, tq), F32),
        ],
        compiler_params=_params("arbitrary", "arbitrary"),
        name="mla_attention",
    )(qt, k, vt, *[w for w, _ in stacked_weights])
    return outs[0], outs[1:]


def _oproj_kernel(a_ref, w_ref, r_ref, o_ref):
    o_ref[...] = r_ref[...] + jnp.dot(a_ref[...], w_ref[...], preferred_element_type=F32)


def _out_proj(a_bf16, w_bf16, resid, *, tm=512, tn=1024):
    s_len, kdim = a_bf16.shape
    d = w_bf16.shape[1]
    tn = min(tn, d)
    assert s_len % tm == 0 and d % tn == 0
    return pl.pallas_call(
        _oproj_kernel,
        grid=(s_len // tm, d // tn),
        in_specs=[
            pl.BlockSpec((tm, kdim), lambda i, j: (i, 0)),
            pl.BlockSpec((kdim, tn), lambda i, j: (0, j)),
            pl.BlockSpec((tm, tn), lambda i, j: (i, j)),
        ],
        out_specs=pl.BlockSpec((tm, tn), lambda i, j: (i, j)),
        out_shape=jax.ShapeDtypeStruct((s_len, d), F32),
        compiler_params=_params("parallel", "parallel"),
        name="attn_out_proj",
    )(a_bf16, w_bf16, resid)


def _trunk(x, a_norm, a_pool_w, a_scale, kv_norm, w_dkv, kv_lat_norm, w_ukv, b_norm, w_dq, q_lat_norm,
           w_uq, w_o, ffn_norm, ffn_w_in, ffn_conv_w, ffn_conv_b, ffn_w_out, final_norm,
           *, pool_ts=256, ffn_tm=512, ffn_tf=256, q_tm=512, kv_tm=256, attn_tq=2048, attn_tk=512, attn_cb=256,
           oproj_tm=512):
    batch, s_len, d = x.shape
    depth = ffn_norm.shape[0]
    n_a = a_norm.shape[0]
    n_b = b_norm.shape[0]
    assert n_a + n_b == depth and n_a >= 1
    lora, n_heads, kvd = w_ukv.shape
    hd = w_uq.shape[-1]
    rope = w_dkv.shape[1] - lora
    nope = hd - rope
    vd = kvd - nope
    half = rope // 2
    hd_pad = -(-hd // LANES) * LANES

    inv_freq = ROPE_THETA ** (-jnp.arange(0, rope, 2, dtype=F32) / rope)
    ang = jnp.arange(s_len, dtype=F32)[:, None] * inv_freq[None, :]
    cos, sin = jnp.cos(ang), jnp.sin(ang)
    cos2 = jnp.concatenate([cos, cos], axis=1)
    sin2 = jnp.concatenate([-sin, sin], axis=1)
    cos_t, sin_t = cos.T, sin.T

    bf = lambda w: w.astype(BF16)
    wd_ext = bf(jnp.concatenate([w_dkv, w_dkv[:, lora + half:], w_dkv[:, lora:lora + half]], axis=1))
    wuk = bf(w_ukv[:, :, :nope].reshape(lora, n_heads * nope))
    wuvt = bf(w_ukv[:, :, nope:].reshape(lora, n_heads * vd).T)
    final_g = final_norm

    outs = []
    ffn_bf16 = {}
    wo_bf16 = {}
    for b in range(batch):
        h = x.reshape(s_len, d) if batch == 1 else x[b]
        shared = None
        for layer in range(depth):
            last = layer == depth - 1
            if layer < n_a:
                h = _pool_layer(h, a_norm[layer], bf(a_pool_w[layer]), a_scale[layer], ts=pool_ts)
            else:
                j = layer - n_a
                qlora = w_dq.shape[2]
                qt = _q_proj(h, b_norm[j], bf(w_dq[j].T), q_lat_norm[j], cos_t, sin_t,
                             bf(w_uq[j].reshape(qlora, n_heads * hd).T),
                             n_heads=n_heads, nope=nope, rope=rope, hd_pad=hd_pad,
                             q_scale=hd ** -0.5 * math.log2(math.e), tile=attn_tk, tm=q_tm)
                first = layer not in ffn_bf16
                to_cast = ((ffn_w_in, layer), (ffn_w_out, layer), (w_o, j)) if first else ()
                o, cast = _attention(qt, shared[0], shared[1], to_cast, vd=vd, tq=attn_tq, tk=attn_tk,
                                     cb=attn_cb)
                if first:
                    ffn_bf16[layer] = cast[:2]
                    wo_bf16[j] = cast[2]
                h = _out_proj(o, wo_bf16[j], h, tm=oproj_tm)
            if layer not in ffn_bf16:
                ffn_bf16[layer] = (bf(ffn_w_in[layer]), None)
            w_in_l, w_out_l = ffn_bf16[layer]
            h = _ffn_layer(h, ffn_norm[layer], w_in_l, ffn_conv_w[layer], ffn_conv_b[layer],
                           ffn_w_out if w_out_l is None else w_out_l, final_g, final=last,
                           w_out_layer=layer if w_out_l is None else None, tm=ffn_tm, tf=ffn_tf)
            if layer == n_a - 1:
                shared = _kv_proj(h, kv_norm, wd_ext, kv_lat_norm, cos2, sin2, wuk, wuvt,
                                  n_heads=n_heads, nope=nope, rope=rope, hd_pad=hd_pad,
                                  ones_rows=BF16_SUBLANES, tile=attn_tk, tm=kv_tm)
        outs.append(h)
    return outs[0].reshape(1, s_len, d) if batch == 1 else jnp.stack(outs, axis=0)


def kernel(x, a_norm, a_pool_w, a_scale, kv_norm, w_dkv, kv_lat_norm, w_ukv, b_norm, w_dq, q_lat_norm,
           w_uq, w_o, ffn_norm, ffn_w_in, ffn_conv_w, ffn_conv_b, ffn_w_out, final_norm):
    return _trunk(x, a_norm, a_pool_w, a_scale, kv_norm, w_dkv, kv_lat_norm, w_ukv, b_norm, w_dq,
                  q_lat_norm, w_uq, w_o, ffn_norm, ffn_w_in, ffn_conv_w, ffn_conv_b, ffn_w_out,
                  final_norm)
```

```python
import functools
import math

import jax
import jax.numpy as jnp
import numpy as np
from jax import lax
from jax.experimental import pallas as pl
from jax.experimental.pallas import tpu as pltpu

EPS = 1e-6
ROPE_THETA = 10000.0
POOL_WINDOWS = (2, 4, 8, 16)
CONV_WIDTH = 3
POOL_HALO = 16
CONV_HALO = 8
NORM_PIECES = 4
CONV_PIECES = 16
MASK_VALUE = -1e30
VMEM_LIMIT_BYTES = 56 * 1024 * 1024
LANES = 128
BF16_SUBLANES = 16

F32 = jnp.float32
BF16 = jnp.bfloat16


def _rms(x, g):
    return x * lax.rsqrt(jnp.mean(x * x, axis=-1, keepdims=True) + EPS) * g


def _params(*sem):
    return pltpu.CompilerParams(dimension_semantics=sem, vmem_limit_bytes=VMEM_LIMIT_BYTES)


def _const_spec(shape):
    nd = len(shape)
    return pl.BlockSpec(shape, lambda *_: (0,) * nd, pipeline_mode=pl.Buffered(1))


def _pool_kernel(x_ref, halo_ref, g_ref, w_ref, sc_ref, o_ref, *, ts, gc):
    i = pl.program_id(0)
    g = g_ref[...]
    x = x_ref[...]
    xn = _rms(x, g)
    hn = jnp.where(i == 0, 0.0, _rms(halo_ref[...], g))
    pos = i * ts + lax.broadcasted_iota(jnp.int32, (ts, 1), 0)
    for gi, w in enumerate(POOL_WINDOWS):
        sl = slice(gi * gc, (gi + 1) * gc)
        xg = xn[:, sl]
        s = jnp.concatenate([hn[:, sl], xg], axis=0)
        k = 1
        while k < w:
            s = s + pltpu.roll(s, k, axis=0)
            k *= 2
        cnt = jnp.minimum(pos + 1, w).astype(F32)
        pooled = s[POOL_HALO:] / cnt - xg
        mixed = jnp.dot(pooled.astype(BF16), w_ref[gi], preferred_element_type=F32)
        o_ref[:, sl] = x[:, sl] + mixed * sc_ref[:, sl]


def _pool_layer(x, norm_g, w_bf16, scale, *, ts=256):
    s_len, d = x.shape
    ng, gc, _ = w_bf16.shape
    assert ng == len(POOL_WINDOWS) and ng * gc == d and s_len % ts == 0 and ts % POOL_HALO == 0
    per = ts // POOL_HALO
    return pl.pallas_call(
        functools.partial(_pool_kernel, ts=ts, gc=gc),
        grid=(s_len // ts,),
        in_specs=[
            pl.BlockSpec((ts, d), lambda i: (i, 0)),
            pl.BlockSpec((POOL_HALO, d), lambda i: (jnp.maximum(i * per - 1, 0), 0)),
            _const_spec((1, d)),
            _const_spec((ng, gc, gc)),
            _const_spec((1, d)),
        ],
        out_specs=pl.BlockSpec((ts, d), lambda i: (i, 0)),
        out_shape=jax.ShapeDtypeStruct((s_len, d), F32),
        compiler_params=_params("parallel"),
        name="pool_layer",
    )(x, x, norm_g.reshape(1, d), w_bf16, scale.reshape(1, d))


def _ffn_kernel(h_ref, g_ref, wg_ref, wv_ref, cwg_ref, cwv_ref, bg_ref, bv_ref, wo_ref, fg_ref,
                o_ref, xn_s, a0_s, a1_s, ug0_s, ug1_s, uv0_s, uv1_s, cg_s, cv_s, *, tm, tn, nf, final):
    i = pl.program_id(0)
    f = pl.program_id(1)
    acts = (a0_s, a1_s)
    ugs = (ug0_s, ug1_s)
    uvs = (uv0_s, uv1_s)

    c = jnp.clip(f - 1, 0, nf - 1)

    @pl.when((i == 0) & (f >= 1) & (f <= nf))
    def _():
        cg_s[c] = jnp.zeros(cg_s.shape[1:], F32)
        cv_s[c] = jnp.zeros(cv_s.shape[1:], F32)

    n_down = o_ref.shape[1] // tn
    rows = tm // CONV_PIECES

    def up_piece(p, which, half):
        u_s, w_ref = ((ugs[p], wg_ref), (uvs[p], wv_ref))[which]
        r = slice(half * (tm // 2), (half + 1) * (tm // 2))
        u_s[r] = jnp.dot(xn_s[r], w_ref[...], preferred_element_type=F32)

    def conv_rows(u_s, cw_ref, b_ref, c_s, r0):
        prev = c_s[c] if r0 == 0 else u_s[r0 - CONV_HALO:r0]
        u = u_s[r0:r0 + rows]
        cat = jnp.concatenate([prev, u], axis=0)
        u1 = pltpu.roll(cat, 1, axis=0)[CONV_HALO:]
        u2 = pltpu.roll(cat, 2, axis=0)[CONV_HALO:]
        cw = cw_ref[...]
        return cw[0:1] * u2 + cw[1:2] * u1 + cw[2:3] * u + b_ref[...]

    def conv_piece(p, k):
        r0 = k * rows
        gate = conv_rows(ugs[p], cwg_ref, bg_ref, cg_s, r0)
        val = conv_rows(uvs[p], cwv_ref, bv_ref, cv_s, r0)
        acts[p][r0:r0 + rows] = (gate * (1.0 / (1.0 + jnp.exp(-gate))) * val).astype(BF16)
        if k == CONV_PIECES - 1:
            cg_s[c] = ugs[p][tm - CONV_HALO:]
            cv_s[c] = uvs[p][tm - CONV_HALO:]

    def down_piece(p, n):
        sl = slice(n * tn, (n + 1) * tn)
        o_ref[:, sl] += jnp.dot(acts[p][...], wo_ref[:, sl], preferred_element_type=F32)

    def run(p_up=None, p_conv=None, p_down=None):
        mxu, vpu = [], []
        if p_up is not None:
            mxu += [functools.partial(up_piece, p_up, w, half) for half in range(2) for w in range(2)]
        if p_down is not None:
            mxu += [functools.partial(down_piece, p_down, n) for n in range(n_down)]
        if p_conv is not None:
            vpu += [functools.partial(conv_piece, p_conv, k) for k in range(CONV_PIECES)]
        for idx in range(max(len(mxu), len(vpu))):
            if idx < len(mxu):
                mxu[idx]()
            if idx < len(vpu):
                vpu[idx]()

    @pl.when(f == 0)
    def _():
        rn = tm // NORM_PIECES

        def norm_rows(k):
            r = slice(k * rn, (k + 1) * rn)
            h = h_ref[r]
            xn_s[r] = _rms(h, g_ref[...]).astype(BF16)
            o_ref[r] = h

        def up_rows(k):
            r = slice(k * rn, (k + 1) * rn)
            ugs[0][r] = jnp.dot(xn_s[r], wg_ref[...], preferred_element_type=F32)
            uvs[0][r] = jnp.dot(xn_s[r], wv_ref[...], preferred_element_type=F32)

        norm_rows(0)
        for k in range(NORM_PIECES):
            if k + 1 < NORM_PIECES:
                norm_rows(k + 1)
            up_rows(k)

    @pl.when(f == 1)
    def _():
        run(p_up=1, p_conv=0)

    for p in range(2):
        @pl.when((f >= 2) & (f < nf) & (f % 2 == p))
        def _():
            run(p_up=p, p_conv=1 - p, p_down=p)

    @pl.when(f == nf)
    def _():
        run(p_conv=(nf - 1) % 2, p_down=nf % 2)

    @pl.when(f == nf + 1)
    def _():
        run(p_down=(nf - 1) % 2)
        if final:
            o_ref[...] = _rms(o_ref[...], fg_ref[...])


def _ffn_layer(h, norm_g, w_in_bf16, conv_w, conv_b, w_out_bf16, final_g, *, final, tm=512, tf=256,
               tn=256):
    s_len, d = h.shape
    dff = w_out_bf16.shape[0]
    tn = min(tn, d)
    assert s_len % tm == 0 and dff % tf == 0 and d % tn == 0 and w_in_bf16.shape == (d, 2 * dff)
    nf = dff // tf
    assert nf >= 2
    conv_b2 = conv_b.reshape(1, 2 * dff)
    up = lambda f: jnp.minimum(f, nf - 1)
    cv = lambda f: jnp.clip(f - 1, 0, nf - 1)
    dn = lambda f: jnp.clip(f - 2, 0, nf - 1)
    return pl.pallas_call(
        functools.partial(_ffn_kernel, tm=tm, tn=tn, nf=nf, final=final),
        grid=(s_len // tm, nf + 2),
        in_specs=[
            pl.BlockSpec((tm, d), lambda i, f: (i, 0), pipeline_mode=pl.Buffered(1)),
            _const_spec((1, d)),
            pl.BlockSpec((d, tf), lambda i, f: (0, up(f))),
            pl.BlockSpec((d, tf), lambda i, f: (0, nf + up(f))),
            pl.BlockSpec((CONV_WIDTH, tf), lambda i, f: (0, cv(f))),
            pl.BlockSpec((CONV_WIDTH, tf), lambda i, f: (0, nf + cv(f))),
            pl.BlockSpec((1, tf), lambda i, f: (0, cv(f))),
            pl.BlockSpec((1, tf), lambda i, f: (0, nf + cv(f))),
            pl.BlockSpec((tf, d), lambda i, f: (dn(f), 0)),
            _const_spec((1, d)),
        ],
        out_specs=pl.BlockSpec((tm, d), lambda i, f: (i, 0)),
        out_shape=jax.ShapeDtypeStruct((s_len, d), F32),
        scratch_shapes=[
            pltpu.VMEM((tm, d), BF16),
            pltpu.VMEM((tm, tf), BF16),
            pltpu.VMEM((tm, tf), BF16),
            pltpu.VMEM((tm, tf), F32),
            pltpu.VMEM((tm, tf), F32),
            pltpu.VMEM((tm, tf), F32),
            pltpu.VMEM((tm, tf), F32),
            pltpu.VMEM((nf, CONV_HALO, tf), F32),
            pltpu.VMEM((nf, CONV_HALO, tf), F32),
        ],
        compiler_params=_params("arbitrary", "arbitrary"),
        name="conv_ffn",
    )(h, norm_g.reshape(1, d), w_in_bf16, w_in_bf16, conv_w, conv_w, conv_b2, conv_b2, w_out_bf16,
      final_g.reshape(1, d))


def _kv_kernel(h_ref, g_ref, wd_ref, lat_ref, cos_ref, sin_ref, wuk_ref, wuvt_ref,
               k_ref, vt_ref, c_s, ct_s, kr_s, *, lora, rope, nope, vd, hg):
    gidx = pl.program_id(1)

    @pl.when(gidx == 0)
    def _():
        hn = _rms(h_ref[...], g_ref[...]).astype(BF16)
        ckv = jnp.dot(hn, wd_ref[...], preferred_element_type=F32)
        c = _rms(ckv[:, :lora], lat_ref[...])
        c_s[...] = c.astype(BF16)
        ct_s[...] = c.T.astype(BF16)
        kr = ckv[:, lora:lora + rope] * cos_ref[...] + ckv[:, lora + rope:] * sin_ref[...]
        kr_s[...] = kr.astype(BF16)

    wuk = wuk_ref[:, pl.ds(pl.multiple_of(gidx * (hg * nope), hg * nope), hg * nope)]
    wuvt = wuvt_ref[pl.ds(pl.multiple_of(gidx * (hg * vd), hg * vd), hg * vd), :]
    kn = jnp.dot(c_s[...], wuk, preferred_element_type=F32)
    vt = jnp.dot(wuvt, ct_s[...], preferred_element_type=F32)
    tm = kn.shape[0]
    hd = nope + rope
    k_pad = k_ref.shape[2] - hd
    ones_rows = vt_ref.shape[2] - vd
    for hh in range(hg):
        k_ref[hh, :, 0:nope] = kn[:, hh * nope:(hh + 1) * nope].astype(BF16)
        k_ref[hh, :, nope:hd] = kr_s[...]
        if k_pad:
            k_ref[hh, :, hd:hd + k_pad] = jnp.zeros((tm, k_pad), BF16)
        vt_ref[hh, 0, 0:vd] = vt[hh * vd:(hh + 1) * vd].astype(BF16)
        vt_ref[hh, 0, vd:vd + ones_rows] = jnp.ones((ones_rows, tm), BF16)


def _kv_proj(h, norm_g, wd_ext_bf16, lat_g, cos2, sin2, wuk_bf16, wuvt_bf16, *, n_heads, nope, rope,
             hd_pad, ones_rows, tile, tm=256, hg=32):
    s_len, d = h.shape
    lora = wuk_bf16.shape[0]
    vd = wuvt_bf16.shape[0] // n_heads
    hg = min(hg, n_heads)
    tm = min(tm, tile)
    per = tile // tm
    assert s_len % tile == 0 and tile % tm == 0 and n_heads % hg == 0
    assert wd_ext_bf16.shape == (d, lora + 2 * rope)
    return pl.pallas_call(
        functools.partial(_kv_kernel, lora=lora, rope=rope, nope=nope, vd=vd, hg=hg),
        grid=(s_len // tm, n_heads // hg),
        in_specs=[
            pl.BlockSpec((tm, d), lambda i, g: (i, 0)),
            _const_spec((1, d)),
            _const_spec((d, lora + 2 * rope)),
            _const_spec((1, lora)),
            pl.BlockSpec((tm, rope), lambda i, g: (i, 0)),
            pl.BlockSpec((tm, rope), lambda i, g: (i, 0)),
            _const_spec((lora, n_heads * nope)),
            _const_spec((n_heads * vd, lora)),
        ],
        out_specs=[
            pl.BlockSpec((hg, tm, hd_pad), lambda i, g: (g, i, 0)),
            pl.BlockSpec((hg, 1, vd + ones_rows, tm), lambda i, g: (g, i // per, 0, i % per)),
        ],
        out_shape=[
            jax.ShapeDtypeStruct((n_heads, s_len, hd_pad), BF16),
            jax.ShapeDtypeStruct((n_heads, s_len // tile, vd + ones_rows, tile), BF16),
        ],
        scratch_shapes=[
            pltpu.VMEM((tm, lora), BF16),
            pltpu.VMEM((lora, tm), BF16),
            pltpu.VMEM((tm, rope), BF16),
        ],
        compiler_params=_params("parallel", "arbitrary"),
        name="kv_proj",
    )(h, norm_g.reshape(1, d), wd_ext_bf16, lat_g.reshape(1, lora), cos2, sin2, wuk_bf16, wuvt_bf16)


def _q_kernel(h_ref, g_ref, wdt_ref, lat_ref, cos_ref, sin_ref, wut_ref, qt_ref, cqt_s,
              *, nope, rope, hg, q_scale):
    gidx = pl.program_id(1)

    @pl.when(gidx == 0)
    def _():
        xn = _rms(h_ref[...], g_ref[...]).astype(BF16)
        cqt = lax.dot_general(wdt_ref[...], xn, (((1,), (1,)), ((), ())),
                              preferred_element_type=F32)
        ms = jnp.mean(cqt * cqt, axis=0, keepdims=True)
        cqt_s[...] = (cqt * lax.rsqrt(ms + EPS) * lat_ref[...]).astype(BF16)

    rows = hg * (nope + rope)
    wut = wut_ref[pl.ds(pl.multiple_of(gidx * rows, rows), rows), :]
    qt = jnp.dot(wut, cqt_s[...], preferred_element_type=F32) * q_scale
    cos_t = cos_ref[...]
    sin_t = sin_ref[...]
    hd = nope + rope
    half = rope // 2
    pad = qt_ref.shape[2] - hd
    for hh in range(hg):
        b = hh * hd
        qt_ref[hh, 0, 0:nope] = qt[b:b + nope].astype(BF16)
        x1 = qt[b + nope:b + nope + half]
        x2 = qt[b + nope + half:b + hd]
        qt_ref[hh, 0, nope:nope + half] = (x1 * cos_t - x2 * sin_t).astype(BF16)
        qt_ref[hh, 0, nope + half:hd] = (x2 * cos_t + x1 * sin_t).astype(BF16)
        if pad:
            qt_ref[hh, 0, hd:hd + pad] = jnp.zeros((pad, qt.shape[1]), BF16)


def _q_proj(h, norm_g, wdqt_bf16, lat_g, cos_t, sin_t, wuqt_bf16, *, n_heads, nope, rope, hd_pad,
            q_scale, tile, tm=512, hg=16):
    s_len, d = h.shape
    qlora = wdqt_bf16.shape[0]
    hd = nope + rope
    hg = min(hg, n_heads)
    tm = min(tm, tile)
    per = tile // tm
    assert s_len % tile == 0 and tile % tm == 0 and n_heads % hg == 0
    assert wuqt_bf16.shape == (n_heads * hd, qlora)
    return pl.pallas_call(
        functools.partial(_q_kernel, nope=nope, rope=rope, hg=hg, q_scale=q_scale),
        grid=(s_len // tm, n_heads // hg),
        in_specs=[
            pl.BlockSpec((tm, d), lambda i, g: (i, 0)),
            _const_spec((1, d)),
            _const_spec((qlora, d)),
            _const_spec((qlora, 1)),
            pl.BlockSpec((rope // 2, tm), lambda i, g: (0, i)),
            pl.BlockSpec((rope // 2, tm), lambda i, g: (0, i)),
            _const_spec((n_heads * hd, qlora)),
        ],
        out_specs=pl.BlockSpec((hg, 1, hd_pad, tm), lambda i, g: (g, i // per, 0, i % per)),
        out_shape=jax.ShapeDtypeStruct((n_heads, s_len // tile, hd_pad, tile), BF16),
        scratch_shapes=[pltpu.VMEM((qlora, tm), BF16)],
        compiler_params=_params("parallel", "arbitrary"),
        name="q_proj",
    )(h, norm_g.reshape(1, d), wdqt_bf16, lat_g.reshape(qlora, 1), cos_t, sin_t, wuqt_bf16)


def _attn_kernel(qt_ref, k_ref, vt_ref, *rest, tq, tk, cb, vd, n_cast):
    w_f32 = rest[:n_cast]
    o_ref = rest[n_cast]
    w_bf16 = rest[n_cast + 1:2 * n_cast + 1]
    s0_ref, s1_ref, cmax0_ref, cmax1_ref, acc_ref, m_ref = rest[2 * n_cast + 1:]
    cmax_buf = (cmax0_ref, cmax1_ref)
    for src, dst in zip(w_f32, w_bf16):
        dst[...] = src[0].astype(BF16)

    i = pl.program_id(1)
    tile = qt_ref.shape[3]
    r = tq // tk
    s_buf = (s0_ref, s1_ref)

    def scores(j, slot, c0, c1, rows=tk):
        kblk = k_ref[0, pl.ds(pl.multiple_of(j * tk, tk), rows), :]
        qt = qt_ref[0, c0 // tile, :, c0 % tile:c0 % tile + (c1 - c0)]
        s = jnp.dot(kblk, qt, preferred_element_type=F32)
        s_buf[slot][0:rows, c0:c1] = s
        cmax_buf[slot][:, c0:c1] = jnp.max(s, axis=0, keepdims=True)

    def visible_rows(c1, key_offset):
        return min(tk, c1 - key_offset)

    def weights_values(j, slot, c0, c1, key_offset):
        rows = tk if key_offset is None else visible_rows(c1, key_offset)
        s = s_buf[slot][0:rows, c0:c1]
        if key_offset is not None and key_offset + rows - 1 > c0:
            kpos = lax.broadcasted_iota(jnp.int32, s.shape, 0) + key_offset
            qpos = lax.broadcasted_iota(jnp.int32, s.shape, 1) + c0
            s = jnp.where(kpos <= qpos, s, MASK_VALUE)
            cmax = jnp.max(s, axis=0, keepdims=True)
        else:
            cmax = cmax_buf[slot][:, c0:c1]
        m = m_ref[:, c0:c1]
        m_new = jnp.maximum(m, cmax)
        alpha = jnp.exp2(m - m_new)
        p = jnp.exp2(s - m_new)
        m_ref[:, c0:c1] = m_new
        vt = vt_ref[0, j, :, 0:rows]
        acc_ref[:, c0:c1] = alpha * acc_ref[:, c0:c1] + jnp.dot(vt, p.astype(BF16),
                                                               preferred_element_type=F32)

    def col_blocks(lo):
        return [(c0, min(c0 + cb, tq)) for c0 in range(lo, tq, cb)]

    def stage(j, slot, lo=0, diagonal=False, next_lo=0):
        cur = col_blocks(lo)
        nxt = col_blocks(next_lo) if next_lo is not None else []
        for idx in range(max(len(cur), len(nxt))):
            if idx < len(nxt):
                scores(j + 1, 1 - slot, *nxt[idx],
                       rows=visible_rows(nxt[idx][1], next_lo) if diagonal else tk)
            if idx < len(cur):
                weights_values(j, slot, *cur[idx], key_offset=lo if diagonal else None)

    m_ref[...] = jnp.full(m_ref.shape, MASK_VALUE, F32)
    acc_ref[...] = jnp.zeros(acc_ref.shape, F32)
    for c0, c1 in col_blocks(0):
        scores(0, 0, c0, c1)

    def group(g, carry):
        for c in range(r):
            stage(g * r + c, c % 2)
        return carry

    lax.fori_loop(0, i, group, 0)
    for c in range(r):
        stage(i * r + c, c % 2, lo=c * tk, diagonal=True, next_lo=(c + 1) * tk if c < r - 1 else None)
    o_ref[...] = (acc_ref[0:vd] / acc_ref[vd:vd + 1]).T.astype(o_ref.dtype)


def _cast_block_rows(n_rows, n_steps):
    for rb in range(BF16_SUBLANES, n_rows + 1, BF16_SUBLANES):
        if n_rows % rb == 0 and n_rows // rb <= n_steps:
            return rb
    return None


def _attention(qt, k, vt, stacked_weights, *, vd, tq=2048, tk=512, cb=512):
    n_heads, n_tiles, hd, tile = qt.shape
    s_len = n_tiles * tile
    vda = vt.shape[2]
    tq = min(tq, s_len)
    cb = min(cb, tq)
    assert k.shape == (n_heads, s_len, hd) and vt.shape == (n_heads, n_tiles, vda, tile) and vda > vd
    assert s_len % tq == 0 and tq % (2 * tk) == 0 and tk == tile and tile % cb == 0
    nq = s_len // tq
    cast_in, cast_out, cast_shapes = [], [], []
    for w, layer in stacked_weights:
        _, n_rows, n_cols = w.shape
        rb = _cast_block_rows(n_rows, n_heads * nq)
        assert rb is not None
        blk = lambda h, i, last=n_rows // rb - 1: jnp.minimum(h * nq + i, last)
        cast_in.append(pl.BlockSpec((1, rb, n_cols), lambda h, i, blk=blk, layer=layer: (layer, blk(h, i), 0)))
        cast_out.append(pl.BlockSpec((rb, n_cols), lambda h, i, blk=blk: (blk(h, i), 0)))
        cast_shapes.append(jax.ShapeDtypeStruct((n_rows, n_cols), BF16))
    outs = pl.pallas_call(
        functools.partial(_attn_kernel, tq=tq, tk=tk, cb=cb, vd=vd, n_cast=len(stacked_weights)),
        grid=(n_heads, nq),
        in_specs=[
            pl.BlockSpec((1, tq // tile, hd, tile), lambda h, i: (h, i, 0, 0)),
            pl.BlockSpec((1, s_len, hd), lambda h, i: (h, 0, 0)),
            pl.BlockSpec((1, n_tiles, vda, tile), lambda h, i: (h, 0, 0, 0)),
        ] + cast_in,
        out_specs=[pl.BlockSpec((tq, vd), lambda h, i: (i, h))] + cast_out,
        out_shape=[jax.ShapeDtypeStruct((s_len, n_heads * vd), BF16)] + cast_shapes,
        scratch_shapes=[
            pltpu.VMEM((tk, tq), F32),
            pltpu.VMEM((tk, tq), F32),
            pltpu.VMEM((1, tq), F32),
            pltpu.VMEM((1, tq), F32),
            pltpu.VMEM((vda, tq), F32),
            pltpu.VMEM((1, tq), F32),
        ],
        compiler_params=_params("arbitrary", "arbitrary"),
        name="mla_attention",
    )(qt, k, vt, *[w for w, _ in stacked_weights])
    return outs[0], outs[1:]


def _oproj_kernel(a_ref, w_ref, r_ref, o_ref):
    o_ref[...] = r_ref[...] + jnp.dot(a_ref[...], w_ref[...], preferred_element_type=F32)


def _out_proj(a_bf16, w_bf16, resid, *, tm=512, tn=1024):
    s_len, kdim = a_bf16.shape
    d = w_bf16.shape[1]
    tn = min(tn, d)
    assert s_len % tm == 0 and d % tn == 0
    return pl.pallas_call(
        _oproj_kernel,
        grid=(s_len // tm, d // tn),
        in_specs=[
            pl.BlockSpec((tm, kdim), lambda i, j: (i, 0)),
            pl.BlockSpec((kdim, tn), lambda i, j: (0, j)),
            pl.BlockSpec((tm, tn), lambda i, j: (i, j)),
        ],
        out_specs=pl.BlockSpec((tm, tn), lambda i, j: (i, j)),
        out_shape=jax.ShapeDtypeStruct((s_len, d), F32),
        compiler_params=_params("parallel", "parallel"),
        name="attn_out_proj",
    )(a_bf16, w_bf16, resid)


def _trunk(x, a_norm, a_pool_w, a_scale, kv_norm, w_dkv, kv_lat_norm, w_ukv, b_norm, w_dq, q_lat_norm,
           w_uq, w_o, ffn_norm, ffn_w_in, ffn_conv_w, ffn_conv_b, ffn_w_out, final_norm,
           *, pool_ts=256, ffn_tm=512, ffn_tf=256, q_tm=512, kv_tm=256, attn_tq=2048, attn_tk=512, attn_cb=256,
           oproj_tm=512):
    batch, s_len, d = x.shape
    depth = ffn_norm.shape[0]
    n_a = a_norm.shape[0]
    n_b = b_norm.shape[0]
    assert n_a + n_b == depth and n_a >= 1
    lora, n_heads, kvd = w_ukv.shape
    hd = w_uq.shape[-1]
    rope = w_dkv.shape[1] - lora
    nope = hd - rope
    vd = kvd - nope
    half = rope // 2
    hd_pad = -(-hd // LANES) * LANES

    expo = -np.arange(0, rope, 2, dtype=np.float32) / np.float32(rope)
    inv_freq = (np.float64(ROPE_THETA) ** expo.astype(np.float64)).astype(np.float32)
    ang = np.arange(s_len, dtype=np.float32)[:, None] * inv_freq[None, :]
    cos = np.cos(ang.astype(np.float64)).astype(np.float32)
    sin = np.sin(ang.astype(np.float64)).astype(np.float32)
    cos2 = jnp.asarray(np.concatenate([cos, cos], axis=1))
    sin2 = jnp.asarray(np.concatenate([-sin, sin], axis=1))
    cos_t, sin_t = jnp.asarray(cos.T.copy()), jnp.asarray(sin.T.copy())

    bf = lambda w: w.astype(BF16)
    wd_ext = bf(jnp.concatenate([w_dkv, w_dkv[:, lora + half:], w_dkv[:, lora:lora + half]], axis=1))
    wuk = bf(w_ukv[:, :, :nope].reshape(lora, n_heads * nope))
    wuvt = bf(w_ukv[:, :, nope:].reshape(lora, n_heads * vd).T)
    final_g = final_norm

    outs = []
    ffn_bf16 = {}
    wo_bf16 = {}
    for b in range(batch):
        h = x.reshape(s_len, d) if batch == 1 else x[b]
        shared = None
        for layer in range(depth):
            last = layer == depth - 1
            if layer < n_a:
                h = _pool_layer(h, a_norm[layer], bf(a_pool_w[layer]), a_scale[layer], ts=pool_ts)
            else:
                j = layer - n_a
                qlora = w_dq.shape[2]
                qt = _q_proj(h, b_norm[j], bf(w_dq[j].T), q_lat_norm[j], cos_t, sin_t,
                             bf(w_uq[j].reshape(qlora, n_heads * hd).T),
                             n_heads=n_heads, nope=nope, rope=rope, hd_pad=hd_pad,
                             q_scale=hd ** -0.5 * math.log2(math.e), tile=attn_tk, tm=q_tm)
                first = layer not in ffn_bf16
                to_cast = ((ffn_w_in, layer), (ffn_w_out, layer), (w_o, j)) if first else ()
                o, cast = _attention(qt, shared[0], shared[1], to_cast, vd=vd, tq=attn_tq, tk=attn_tk,
                                     cb=attn_cb)
                if first:
                    ffn_bf16[layer] = cast[:2]
                    wo_bf16[j] = cast[2]
                h = _out_proj(o, wo_bf16[j], h, tm=oproj_tm)
            if layer not in ffn_bf16:
                ffn_bf16[layer] = (bf(ffn_w_in[layer]), bf(ffn_w_out[layer]))
            h = _ffn_layer(h, ffn_norm[layer], ffn_bf16[layer][0], ffn_conv_w[layer], ffn_conv_b[layer],
                           ffn_bf16[layer][1], final_g, final=last, tm=ffn_tm, tf=ffn_tf)
            if layer == n_a - 1:
                shared = _kv_proj(h, kv_norm, wd_ext, kv_lat_norm, cos2, sin2, wuk, wuvt,
                                  n_heads=n_heads, nope=nope, rope=rope, hd_pad=hd_pad,
                                  ones_rows=BF16_SUBLANES, tile=attn_tk, tm=kv_tm)
        outs.append(h)
    return outs[0].reshape(1, s_len, d) if batch == 1 else jnp.stack(outs, axis=0)


def kernel(x, a_norm, a_pool_w, a_scale, kv_norm, w_dkv, kv_lat_norm, w_ukv, b_norm, w_dq, q_lat_norm,
           w_uq, w_o, ffn_norm, ffn_w_in, ffn_conv_w, ffn_conv_b, ffn_w_out, final_norm):
    return _trunk(x, a_norm, a_pool_w, a_scale, kv_norm, w_dkv, kv_lat_norm, w_ukv, b_norm, w_dq,
                  q_lat_norm, w_uq, w_o, ffn_norm, ffn_w_in, ffn_conv_w, ffn_conv_b, ffn_w_out,
                  final_norm)
```

```python
import functools
import math

import jax
import jax.numpy as jnp
import numpy as np
from jax import lax
from jax.experimental import pallas as pl
from jax.experimental.pallas import tpu as pltpu

EPS = 1e-6
ROPE_THETA = 10000.0
POOL_WINDOWS = (2, 4, 8, 16)
CONV_WIDTH = 3
POOL_HALO = 16
CONV_HALO = 8
NORM_PIECES = 4
CONV_PIECES = 16
MASK_VALUE = -1e30
VMEM_LIMIT_BYTES = 56 * 1024 * 1024
LANES = 128
BF16_SUBLANES = 16

F32 = jnp.float32
BF16 = jnp.bfloat16


def _rms(x, g):
    return x * lax.rsqrt(jnp.mean(x * x, axis=-1, keepdims=True) + EPS) * g


def _params(*sem):
    return pltpu.CompilerParams(dimension_semantics=sem, vmem_limit_bytes=VMEM_LIMIT_BYTES)


def _const_spec(shape):
    nd = len(shape)
    return pl.BlockSpec(shape, lambda *_: (0,) * nd, pipeline_mode=pl.Buffered(1))


def _pool_kernel(x_ref, halo_ref, g_ref, w_ref, sc_ref, o_ref, *, ts, gc):
    i = pl.program_id(0)
    g = g_ref[...]
    x = x_ref[...]
    xn = _rms(x, g)
    hn = jnp.where(i == 0, 0.0, _rms(halo_ref[...], g))
    pos = i * ts + lax.broadcasted_iota(jnp.int32, (ts, 1), 0)
    for gi, w in enumerate(POOL_WINDOWS):
        sl = slice(gi * gc, (gi + 1) * gc)
        xg = xn[:, sl]
        s = jnp.concatenate([hn[:, sl], xg], axis=0)
        k = 1
        while k < w:
            s = s + pltpu.roll(s, k, axis=0)
            k *= 2
        cnt = jnp.minimum(pos + 1, w).astype(F32)
        pooled = s[POOL_HALO:] / cnt - xg
        mixed = jnp.dot(pooled.astype(BF16), w_ref[gi], preferred_element_type=F32)
        o_ref[:, sl] = x[:, sl] + mixed * sc_ref[:, sl]


def _pool_layer(x, norm_g, w_bf16, scale, *, ts=256):
    s_len, d = x.shape
    ng, gc, _ = w_bf16.shape
    assert ng == len(POOL_WINDOWS) and ng * gc == d and s_len % ts == 0 and ts % POOL_HALO == 0
    per = ts // POOL_HALO
    return pl.pallas_call(
        functools.partial(_pool_kernel, ts=ts, gc=gc),
        grid=(s_len // ts,),
        in_specs=[
            pl.BlockSpec((ts, d), lambda i: (i, 0)),
            pl.BlockSpec((POOL_HALO, d), lambda i: (jnp.maximum(i * per - 1, 0), 0)),
            _const_spec((1, d)),
            _const_spec((ng, gc, gc)),
            _const_spec((1, d)),
        ],
        out_specs=pl.BlockSpec((ts, d), lambda i: (i, 0)),
        out_shape=jax.ShapeDtypeStruct((s_len, d), F32),
        compiler_params=_params("parallel"),
        name="pool_layer",
    )(x, x, norm_g.reshape(1, d), w_bf16, scale.reshape(1, d))


def _ffn_kernel(h_ref, g_ref, wg_ref, wv_ref, cwg_ref, cwv_ref, bg_ref, bv_ref, wo_ref, fg_ref,
                o_ref, xn_s, a0_s, a1_s, ug0_s, ug1_s, uv0_s, uv1_s, cg_s, cv_s, *, tm, tn, nf, final):
    i = pl.program_id(0)
    f = pl.program_id(1)
    acts = (a0_s, a1_s)
    ugs = (ug0_s, ug1_s)
    uvs = (uv0_s, uv1_s)

    c = jnp.clip(f - 1, 0, nf - 1)

    @pl.when((i == 0) & (f >= 1) & (f <= nf))
    def _():
        cg_s[c] = jnp.zeros(cg_s.shape[1:], F32)
        cv_s[c] = jnp.zeros(cv_s.shape[1:], F32)

    n_down = o_ref.shape[1] // tn
    rows = tm // CONV_PIECES

    def up_piece(p, which, half):
        u_s, w_ref = ((ugs[p], wg_ref), (uvs[p], wv_ref))[which]
        r = slice(half * (tm // 2), (half + 1) * (tm // 2))
        u_s[r] = jnp.dot(xn_s[r], w_ref[...], preferred_element_type=F32)

    def conv_rows(u_s, cw_ref, b_ref, c_s, r0):
        prev = c_s[c] if r0 == 0 else u_s[r0 - CONV_HALO:r0]
        u = u_s[r0:r0 + rows]
        cat = jnp.concatenate([prev, u], axis=0)
        u1 = pltpu.roll(cat, 1, axis=0)[CONV_HALO:]
        u2 = pltpu.roll(cat, 2, axis=0)[CONV_HALO:]
        cw = cw_ref[...]
        return cw[0:1] * u2 + cw[1:2] * u1 + cw[2:3] * u + b_ref[...]

    def conv_piece(p, k):
        r0 = k * rows
        gate = conv_rows(ugs[p], cwg_ref, bg_ref, cg_s, r0)
        val = conv_rows(uvs[p], cwv_ref, bv_ref, cv_s, r0)
        acts[p][r0:r0 + rows] = (gate * (1.0 / (1.0 + jnp.exp(-gate))) * val).astype(BF16)
        if k == CONV_PIECES - 1:
            cg_s[c] = ugs[p][tm - CONV_HALO:]
            cv_s[c] = uvs[p][tm - CONV_HALO:]

    def down_piece(p, n):
        sl = slice(n * tn, (n + 1) * tn)
        o_ref[:, sl] += jnp.dot(acts[p][...], wo_ref[:, sl], preferred_element_type=F32)

    def run(p_up=None, p_conv=None, p_down=None):
        mxu, vpu = [], []
        if p_up is not None:
            mxu += [functools.partial(up_piece, p_up, w, half) for half in range(2) for w in range(2)]
        if p_down is not None:
            mxu += [functools.partial(down_piece, p_down, n) for n in range(n_down)]
        if p_conv is not None:
            vpu += [functools.partial(conv_piece, p_conv, k) for k in range(CONV_PIECES)]
        for idx in range(max(len(mxu), len(vpu))):
            if idx < len(mxu):
                mxu[idx]()
            if idx < len(vpu):
                vpu[idx]()

    @pl.when(f == 0)
    def _():
        rn = tm // NORM_PIECES

        def norm_rows(k):
            r = slice(k * rn, (k + 1) * rn)
            h = h_ref[r]
            xn_s[r] = _rms(h, g_ref[...]).astype(BF16)
            o_ref[r] = h

        def up_rows(k):
            r = slice(k * rn, (k + 1) * rn)
            ugs[0][r] = jnp.dot(xn_s[r], wg_ref[...], preferred_element_type=F32)
            uvs[0][r] = jnp.dot(xn_s[r], wv_ref[...], preferred_element_type=F32)

        norm_rows(0)
        for k in range(NORM_PIECES):
            if k + 1 < NORM_PIECES:
                norm_rows(k + 1)
            up_rows(k)

    @pl.when(f == 1)
    def _():
        run(p_up=1, p_conv=0)

    for p in range(2):
        @pl.when((f >= 2) & (f < nf) & (f % 2 == p))
        def _():
            run(p_up=p, p_conv=1 - p, p_down=p)

    @pl.when(f == nf)
    def _():
        run(p_conv=(nf - 1) % 2, p_down=nf % 2)

    @pl.when(f == nf + 1)
    def _():
        run(p_down=(nf - 1) % 2)
        if final:
            o_ref[...] = _rms(o_ref[...], fg_ref[...])


def _ffn_layer(h, norm_g, w_in_bf16, conv_w, conv_b, w_out_bf16, final_g, *, final, tm=512, tf=256,
               tn=256):
    s_len, d = h.shape
    dff = w_out_bf16.shape[0]
    tn = min(tn, d)
    assert s_len % tm == 0 and dff % tf == 0 and d % tn == 0 and w_in_bf16.shape == (d, 2 * dff)
    nf = dff // tf
    assert nf >= 2
    conv_b2 = conv_b.reshape(1, 2 * dff)
    up = lambda f: jnp.minimum(f, nf - 1)
    cv = lambda f: jnp.clip(f - 1, 0, nf - 1)
    dn = lambda f: jnp.clip(f - 2, 0, nf - 1)
    return pl.pallas_call(
        functools.partial(_ffn_kernel, tm=tm, tn=tn, nf=nf, final=final),
        grid=(s_len // tm, nf + 2),
        in_specs=[
            pl.BlockSpec((tm, d), lambda i, f: (i, 0), pipeline_mode=pl.Buffered(1)),
            _const_spec((1, d)),
            pl.BlockSpec((d, tf), lambda i, f: (0, up(f))),
            pl.BlockSpec((d, tf), lambda i, f: (0, nf + up(f))),
            pl.BlockSpec((CONV_WIDTH, tf), lambda i, f: (0, cv(f))),
            pl.BlockSpec((CONV_WIDTH, tf), lambda i, f: (0, nf + cv(f))),
            pl.BlockSpec((1, tf), lambda i, f: (0, cv(f))),
            pl.BlockSpec((1, tf), lambda i, f: (0, nf + cv(f))),
            pl.BlockSpec((tf, d), lambda i, f: (dn(f), 0)),
            _const_spec((1, d)),
        ],
        out_specs=pl.BlockSpec((tm, d), lambda i, f: (i, 0)),
        out_shape=jax.ShapeDtypeStruct((s_len, d), F32),
        scratch_shapes=[
            pltpu.VMEM((tm, d), BF16),
            pltpu.VMEM((tm, tf), BF16),
            pltpu.VMEM((tm, tf), BF16),
            pltpu.VMEM((tm, tf), F32),
            pltpu.VMEM((tm, tf), F32),
            pltpu.VMEM((tm, tf), F32),
            pltpu.VMEM((tm, tf), F32),
            pltpu.VMEM((nf, CONV_HALO, tf), F32),
            pltpu.VMEM((nf, CONV_HALO, tf), F32),
        ],
        compiler_params=_params("arbitrary", "arbitrary"),
        name="conv_ffn",
    )(h, norm_g.reshape(1, d), w_in_bf16, w_in_bf16, conv_w, conv_w, conv_b2, conv_b2, w_out_bf16,
      final_g.reshape(1, d))


def _kv_kernel(h_ref, g_ref, wd_ref, lat_ref, cos_ref, sin_ref, wuk_ref, wuvt_ref,
               k_ref, vt_ref, c_s, ct_s, kr_s, *, lora, rope, nope, vd, hg):
    gidx = pl.program_id(1)

    @pl.when(gidx == 0)
    def _():
        hn = _rms(h_ref[...], g_ref[...]).astype(BF16)
        ckv = jnp.dot(hn, wd_ref[...], preferred_element_type=F32)
        c = _rms(ckv[:, :lora], lat_ref[...])
        c_s[...] = c.astype(BF16)
        ct_s[...] = c.T.astype(BF16)
        kr = ckv[:, lora:lora + rope] * cos_ref[...] + ckv[:, lora + rope:] * sin_ref[...]
        kr_s[...] = kr.astype(BF16)

    wuk = wuk_ref[:, pl.ds(pl.multiple_of(gidx * (hg * nope), hg * nope), hg * nope)]
    wuvt = wuvt_ref[pl.ds(pl.multiple_of(gidx * (hg * vd), hg * vd), hg * vd), :]
    kn = jnp.dot(c_s[...], wuk, preferred_element_type=F32)
    vt = jnp.dot(wuvt, ct_s[...], preferred_element_type=F32)
    tm = kn.shape[0]
    hd = nope + rope
    k_pad = k_ref.shape[2] - hd
    ones_rows = vt_ref.shape[2] - vd
    for hh in range(hg):
        k_ref[hh, :, 0:nope] = kn[:, hh * nope:(hh + 1) * nope].astype(BF16)
        k_ref[hh, :, nope:hd] = kr_s[...]
        if k_pad:
            k_ref[hh, :, hd:hd + k_pad] = jnp.zeros((tm, k_pad), BF16)
        vt_ref[hh, 0, 0:vd] = vt[hh * vd:(hh + 1) * vd].astype(BF16)
        vt_ref[hh, 0, vd:vd + ones_rows] = jnp.ones((ones_rows, tm), BF16)


def _kv_proj(h, norm_g, wd_ext_bf16, lat_g, cos2, sin2, wuk_bf16, wuvt_bf16, *, n_heads, nope, rope,
             hd_pad, ones_rows, tile, tm=256, hg=32):
    s_len, d = h.shape
    lora = wuk_bf16.shape[0]
    vd = wuvt_bf16.shape[0] // n_heads
    hg = min(hg, n_heads)
    tm = min(tm, tile)
    per = tile // tm
    assert s_len % tile == 0 and tile % tm == 0 and n_heads % hg == 0
    assert wd_ext_bf16.shape == (d, lora + 2 * rope)
    return pl.pallas_call(
        functools.partial(_kv_kernel, lora=lora, rope=rope, nope=nope, vd=vd, hg=hg),
        grid=(s_len // tm, n_heads // hg),
        in_specs=[
            pl.BlockSpec((tm, d), lambda i, g: (i, 0)),
            _const_spec((1, d)),
            _const_spec((d, lora + 2 * rope)),
            _const_spec((1, lora)),
            pl.BlockSpec((tm, rope), lambda i, g: (i, 0)),
            pl.BlockSpec((tm, rope), lambda i, g: (i, 0)),
            _const_spec((lora, n_heads * nope)),
            _const_spec((n_heads * vd, lora)),
        ],
        out_specs=[
            pl.BlockSpec((hg, tm, hd_pad), lambda i, g: (g, i, 0)),
            pl.BlockSpec((hg, 1, vd + ones_rows, tm), lambda i, g: (g, i // per, 0, i % per)),
        ],
        out_shape=[
            jax.ShapeDtypeStruct((n_heads, s_len, hd_pad), BF16),
            jax.ShapeDtypeStruct((n_heads, s_len // tile, vd + ones_rows, tile), BF16),
        ],
        scratch_shapes=[
            pltpu.VMEM((tm, lora), BF16),
            pltpu.VMEM((lora, tm), BF16),
            pltpu.VMEM((tm, rope), BF16),
        ],
        compiler_params=_params("parallel", "arbitrary"),
        name="kv_proj",
    )(h, norm_g.reshape(1, d), wd_ext_bf16, lat_g.reshape(1, lora), cos2, sin2, wuk_bf16, wuvt_bf16)


def _q_kernel(h_ref, g_ref, wdt_ref, lat_ref, cos_ref, sin_ref, wut_ref, qt_ref, cqt_s,
              *, nope, rope, hg, q_scale):
    gidx = pl.program_id(1)

    @pl.when(gidx == 0)
    def _():
        xn = _rms(h_ref[...], g_ref[...]).astype(BF16)
        cqt = lax.dot_general(wdt_ref[...], xn, (((1,), (1,)), ((), ())),
                              preferred_element_type=F32)
        ms = jnp.mean(cqt * cqt, axis=0, keepdims=True)
        cqt_s[...] = (cqt * lax.rsqrt(ms + EPS) * lat_ref[...]).astype(BF16)

    rows = hg * (nope + rope)
    wut = wut_ref[pl.ds(pl.multiple_of(gidx * rows, rows), rows), :]
    qt = jnp.dot(wut, cqt_s[...], preferred_element_type=F32) * q_scale
    cos_t = cos_ref[...]
    sin_t = sin_ref[...]
    hd = nope + rope
    half = rope // 2
    pad = qt_ref.shape[2] - hd
    for hh in range(hg):
        b = hh * hd
        qt_ref[hh, 0, 0:nope] = qt[b:b + nope].astype(BF16)
        x1 = qt[b + nope:b + nope + half]
        x2 = qt[b + nope + half:b + hd]
        qt_ref[hh, 0, nope:nope + half] = (x1 * cos_t - x2 * sin_t).astype(BF16)
        qt_ref[hh, 0, nope + half:hd] = (x2 * cos_t + x1 * sin_t).astype(BF16)
        if pad:
            qt_ref[hh, 0, hd:hd + pad] = jnp.zeros((pad, qt.shape[1]), BF16)


def _q_proj(h, norm_g, wdqt_bf16, lat_g, cos_t, sin_t, wuqt_bf16, *, n_heads, nope, rope, hd_pad,
            q_scale, tile, tm=512, hg=16):
    s_len, d = h.shape
    qlora = wdqt_bf16.shape[0]
    hd = nope + rope
    hg = min(hg, n_heads)
    tm = min(tm, tile)
    per = tile // tm
    assert s_len % tile == 0 and tile % tm == 0 and n_heads % hg == 0
    assert wuqt_bf16.shape == (n_heads * hd, qlora)
    return pl.pallas_call(
        functools.partial(_q_kernel, nope=nope, rope=rope, hg=hg, q_scale=q_scale),
        grid=(s_len // tm, n_heads // hg),
        in_specs=[
            pl.BlockSpec((tm, d), lambda i, g: (i, 0)),
            _const_spec((1, d)),
            _const_spec((qlora, d)),
            _const_spec((qlora, 1)),
            pl.BlockSpec((rope // 2, tm), lambda i, g: (0, i)),
            pl.BlockSpec((rope // 2, tm), lambda i, g: (0, i)),
            _const_spec((n_heads * hd, qlora)),
        ],
        out_specs=pl.BlockSpec((hg, 1, hd_pad, tm), lambda i, g: (g, i // per, 0, i % per)),
        out_shape=jax.ShapeDtypeStruct((n_heads, s_len // tile, hd_pad, tile), BF16),
        scratch_shapes=[pltpu.VMEM((qlora, tm), BF16)],
        compiler_params=_params("parallel", "arbitrary"),
        name="q_proj",
    )(h, norm_g.reshape(1, d), wdqt_bf16, lat_g.reshape(qlora, 1), cos_t, sin_t, wuqt_bf16)


def _attn_kernel(qt_ref, k_ref, vt_ref, *rest, tq, tk, cb, vd, n_cast):
    w_f32 = rest[:n_cast]
    o_ref = rest[n_cast]
    w_bf16 = rest[n_cast + 1:2 * n_cast + 1]
    s0_ref, s1_ref, cmax0_ref, cmax1_ref, acc_ref, m_ref = rest[2 * n_cast + 1:]
    cmax_buf = (cmax0_ref, cmax1_ref)

    i = pl.program_id(1)
    tile = qt_ref.shape[3]
    r = tq // tk
    s_buf = (s0_ref, s1_ref)

    def scores(j, slot, c0, c1, rows=tk):
        kblk = k_ref[0, pl.ds(pl.multiple_of(j * tk, tk), rows), :]
        qt = qt_ref[0, c0 // tile, :, c0 % tile:c0 % tile + (c1 - c0)]
        s = jnp.dot(kblk, qt, preferred_element_type=F32)
        s_buf[slot][0:rows, c0:c1] = s
        cmax_buf[slot][:, c0:c1] = jnp.max(s, axis=0, keepdims=True)

    def visible_rows(c1, key_offset):
        return min(tk, c1 - key_offset)

    def weights_values(j, slot, c0, c1, key_offset):
        rows = tk if key_offset is None else visible_rows(c1, key_offset)
        s = s_buf[slot][0:rows, c0:c1]
        if key_offset is not None and key_offset + rows - 1 > c0:
            kpos = lax.broadcasted_iota(jnp.int32, s.shape, 0) + key_offset
            qpos = lax.broadcasted_iota(jnp.int32, s.shape, 1) + c0
            s = jnp.where(kpos <= qpos, s, MASK_VALUE)
            cmax = jnp.max(s, axis=0, keepdims=True)
        else:
            cmax = cmax_buf[slot][:, c0:c1]
        m = m_ref[:, c0:c1]
        m_new = jnp.maximum(m, cmax)
        alpha = jnp.exp2(m - m_new)
        p = jnp.exp2(s - m_new)
        m_ref[:, c0:c1] = m_new
        vt = vt_ref[0, j, :, 0:rows]
        acc_ref[:, c0:c1] = alpha * acc_ref[:, c0:c1] + jnp.dot(vt, p.astype(BF16),
                                                               preferred_element_type=F32)

    def col_blocks(lo):
        return [(c0, min(c0 + cb, tq)) for c0 in range(lo, tq, cb)]

    def stage(j, slot, lo=0, diagonal=False, next_lo=0):
        cur = col_blocks(lo)
        nxt = col_blocks(next_lo) if next_lo is not None else []
        for idx in range(max(len(cur), len(nxt))):
            if idx < len(nxt):
                scores(j + 1, 1 - slot, *nxt[idx],
                       rows=visible_rows(nxt[idx][1], next_lo) if diagonal else tk)
            if idx < len(cur):
                weights_values(j, slot, *cur[idx], key_offset=lo if diagonal else None)

    m_ref[...] = jnp.full(m_ref.shape, MASK_VALUE, F32)
    acc_ref[...] = jnp.zeros(acc_ref.shape, F32)
    for c0, c1 in col_blocks(0):
        scores(0, 0, c0, c1)

    def group(g, carry):
        for c in range(r):
            stage(g * r + c, c % 2)
        return carry

    lax.fori_loop(0, i, group, 0)
    for c in range(r):
        stage(i * r + c, c % 2, lo=c * tk, diagonal=True, next_lo=(c + 1) * tk if c < r - 1 else None)
    o_ref[...] = (acc_ref[0:vd] / acc_ref[vd:vd + 1]).T.astype(o_ref.dtype)
    for src, dst in zip(w_f32, w_bf16):
        dst[...] = src[0].astype(BF16)


def _cast_block_rows(n_rows, n_steps):
    for rb in range(BF16_SUBLANES, n_rows + 1, BF16_SUBLANES):
        if n_rows % rb == 0 and n_rows // rb <= n_steps:
            return rb
    return None


def _attention(qt, k, vt, stacked_weights, *, vd, tq=2048, tk=512, cb=512):
    n_heads, n_tiles, hd, tile = qt.shape
    s_len = n_tiles * tile
    vda = vt.shape[2]
    tq = min(tq, s_len)
    cb = min(cb, tq)
    assert k.shape == (n_heads, s_len, hd) and vt.shape == (n_heads, n_tiles, vda, tile) and vda > vd
    assert s_len % tq == 0 and tq % (2 * tk) == 0 and tk == tile and tile % cb == 0
    nq = s_len // tq
    cast_in, cast_out, cast_shapes = [], [], []
    for w, layer in stacked_weights:
        _, n_rows, n_cols = w.shape
        rb = _cast_block_rows(n_rows, n_heads * nq)
        assert rb is not None
        blk = lambda h, i, last=n_rows // rb - 1: jnp.minimum(h * nq + i, last)
        cast_in.append(pl.BlockSpec((1, rb, n_cols), lambda h, i, blk=blk, layer=layer: (layer, blk(h, i), 0)))
        cast_out.append(pl.BlockSpec((rb, n_cols), lambda h, i, blk=blk: (blk(h, i), 0)))
        cast_shapes.append(jax.ShapeDtypeStruct((n_rows, n_cols), BF16))
    outs = pl.pallas_call(
        functools.partial(_attn_kernel, tq=tq, tk=tk, cb=cb, vd=vd, n_cast=len(stacked_weights)),
        grid=(n_heads, nq),
        in_specs=[
            pl.BlockSpec((1, tq // tile, hd, tile), lambda h, i: (h, i, 0, 0)),
            pl.BlockSpec((1, s_len, hd), lambda h, i: (h, 0, 0)),
            pl.BlockSpec((1, n_tiles, vda, tile), lambda h, i: (h, 0, 0, 0)),
        ] + cast_in,
        out_specs=[pl.BlockSpec((tq, vd), lambda h, i: (i, h))] + cast_out,
        out_shape=[jax.ShapeDtypeStruct((s_len, n_heads * vd), BF16)] + cast_shapes,
        scratch_shapes=[
            pltpu.VMEM((tk, tq), F32),
            pltpu.VMEM((tk, tq), F32),
            pltpu.VMEM((1, tq), F32),
            pltpu.VMEM((1, tq), F32),
            pltpu.VMEM((vda, tq), F32),
            pltpu.VMEM((1, tq), F32),
        ],
        compiler_params=_params("arbitrary", "arbitrary"),
        name="mla_attention",
    )(qt, k, vt, *[w for w, _ in stacked_weights])
    return outs[0], outs[1:]


def _oproj_kernel(a_ref, w_ref, r_ref, o_ref):
    o_ref[...] = r_ref[...] + jnp.dot(a_ref[...], w_ref[...], preferred_element_type=F32)


def _out_proj(a_bf16, w_bf16, resid, *, tm=512, tn=1024):
    s_len, kdim = a_bf16.shape
    d = w_bf16.shape[1]
    tn = min(tn, d)
    assert s_len % tm == 0 and d % tn == 0
    return pl.pallas_call(
        _oproj_kernel,
        grid=(s_len // tm, d // tn),
        in_specs=[
            pl.BlockSpec((tm, kdim), lambda i, j: (i, 0)),
            pl.BlockSpec((kdim, tn), lambda i, j: (0, j)),
            pl.BlockSpec((tm, tn), lambda i, j: (i, j)),
        ],
        out_specs=pl.BlockSpec((tm, tn), lambda i, j: (i, j)),
        out_shape=jax.ShapeDtypeStruct((s_len, d), F32),
        compiler_params=_params("parallel", "parallel"),
        name="attn_out_proj",
    )(a_bf16, w_bf16, resid)


def _trunk(x, a_norm, a_pool_w, a_scale, kv_norm, w_dkv, kv_lat_norm, w_ukv, b_norm, w_dq, q_lat_norm,
           w_uq, w_o, ffn_norm, ffn_w_in, ffn_conv_w, ffn_conv_b, ffn_w_out, final_norm,
           *, pool_ts=256, ffn_tm=512, ffn_tf=256, q_tm=512, kv_tm=256, attn_tq=2048, attn_tk=512, attn_cb=256,
           oproj_tm=512):
    batch, s_len, d = x.shape
    depth = ffn_norm.shape[0]
    n_a = a_norm.shape[0]
    n_b = b_norm.shape[0]
    assert n_a + n_b == depth and n_a >= 1
    lora, n_heads, kvd = w_ukv.shape
    hd = w_uq.shape[-1]
    rope = w_dkv.shape[1] - lora
    nope = hd - rope
    vd = kvd - nope
    half = rope // 2
    hd_pad = -(-hd // LANES) * LANES

    expo = -np.arange(0, rope, 2, dtype=np.float32) / np.float32(rope)
    inv_freq = (np.float64(ROPE_THETA) ** expo.astype(np.float64)).astype(np.float32)
    ang = np.arange(s_len, dtype=np.float32)[:, None] * inv_freq[None, :]
    cos = np.cos(ang.astype(np.float64)).astype(np.float32)
    sin = np.sin(ang.astype(np.float64)).astype(np.float32)
    cos2 = jnp.asarray(np.concatenate([cos, cos], axis=1))
    sin2 = jnp.asarray(np.concatenate([-sin, sin], axis=1))
    cos_t, sin_t = jnp.asarray(cos.T.copy()), jnp.asarray(sin.T.copy())

    bf = lambda w: w.astype(BF16)
    wd_ext = bf(jnp.concatenate([w_dkv, w_dkv[:, lora + half:], w_dkv[:, lora:lora + half]], axis=1))
    wuk = bf(w_ukv[:, :, :nope].reshape(lora, n_heads * nope))
    wuvt = bf(w_ukv[:, :, nope:].reshape(lora, n_heads * vd).T)
    final_g = final_norm

    outs = []
    ffn_bf16 = {}
    wo_bf16 = {}
    for b in range(batch):
        h = x.reshape(s_len, d) if batch == 1 else x[b]
        shared = None
        for layer in range(depth):
            last = layer == depth - 1
            if layer < n_a:
                h = _pool_layer(h, a_norm[layer], bf(a_pool_w[layer]), a_scale[layer], ts=pool_ts)
            else:
                j = layer - n_a
                qlora = w_dq.shape[2]
                qt = _q_proj(h, b_norm[j], bf(w_dq[j].T), q_lat_norm[j], cos_t, sin_t,
                             bf(w_uq[j].reshape(qlora, n_heads * hd).T),
                             n_heads=n_heads, nope=nope, rope=rope, hd_pad=hd_pad,
                             q_scale=hd ** -0.5 * math.log2(math.e), tile=attn_tk, tm=q_tm)
                first = layer not in ffn_bf16
                to_cast = ((ffn_w_in, layer), (ffn_w_out, layer), (w_o, j)) if first else ()
                o, cast = _attention(qt, shared[0], shared[1], to_cast, vd=vd, tq=attn_tq, tk=attn_tk,
                                     cb=attn_cb)
                if first:
                    ffn_bf16[layer] = cast[:2]
                    wo_bf16[j] = cast[2]
                h = _out_proj(o, wo_bf16[j], h, tm=oproj_tm)
            if layer not in ffn_bf16:
                ffn_bf16[layer] = (bf(ffn_w_in[layer]), bf(ffn_w_out[layer]))
            h = _ffn_layer(h, ffn_norm[layer], ffn_bf16[layer][0], ffn_conv_w[layer], ffn_conv_b[layer],
                           ffn_bf16[layer][1], final_g, final=last, tm=ffn_tm, tf=ffn_tf)
            if layer == n_a - 1:
                shared = _kv_proj(h, kv_norm, wd_ext, kv_lat_norm, cos2, sin2, wuk, wuvt,
                                  n_heads=n_heads, nope=nope, rope=rope, hd_pad=hd_pad,
                                  ones_rows=BF16_SUBLANES, tile=attn_tk, tm=kv_tm)
        outs.append(h)
    return outs[0].reshape(1, s_len, d) if batch == 1 else jnp.stack(outs, axis=0)


def kernel(x, a_norm, a_pool_w, a_scale, kv_norm, w_dkv, kv_lat_norm, w_ukv, b_norm, w_dq, q_lat_norm,
           w_uq, w_o, ffn_norm, ffn_w_in, ffn_conv_w, ffn_conv_b, ffn_w_out, final_norm):
    return _trunk(x, a_norm, a_pool_w, a_scale, kv_norm, w_dkv, kv_lat_norm, w_ukv, b_norm, w_dq,
                  q_lat_norm, w_uq, w_o, ffn_norm, ffn_w_in, ffn_conv_w, ffn_conv_b, ffn_w_out,
                  final_norm)
```

```python
import functools
import math

import jax
import jax.numpy as jnp
import numpy as np
from jax import lax
from jax.experimental import pallas as pl
from jax.experimental.pallas import tpu as pltpu

EPS = 1e-6
ROPE_THETA = 10000.0
POOL_WINDOWS = (2, 4, 8, 16)
CONV_WIDTH = 3
POOL_HALO = 16
CONV_HALO = 8
NORM_PIECES = 4
CONV_PIECES = 16
MASK_VALUE = -1e30
VMEM_LIMIT_BYTES = 56 * 1024 * 1024
LANES = 128
BF16_SUBLANES = 16

F32 = jnp.float32
BF16 = jnp.bfloat16


def _rms(x, g):
    return x * lax.rsqrt(jnp.mean(x * x, axis=-1, keepdims=True) + EPS) * g


def _params(*sem):
    return pltpu.CompilerParams(dimension_semantics=sem, vmem_limit_bytes=VMEM_LIMIT_BYTES)


def _const_spec(shape):
    nd = len(shape)
    return pl.BlockSpec(shape, lambda *_: (0,) * nd, pipeline_mode=pl.Buffered(1))


def _pool_kernel(x_ref, halo_ref, g_ref, w_ref, sc_ref, o_ref, *, ts, gc):
    i = pl.program_id(0)
    g = g_ref[...]
    x = x_ref[...]
    xn = _rms(x, g)
    hn = jnp.where(i == 0, 0.0, _rms(halo_ref[...], g))
    pos = i * ts + lax.broadcasted_iota(jnp.int32, (ts, 1), 0)
    for gi, w in enumerate(POOL_WINDOWS):
        sl = slice(gi * gc, (gi + 1) * gc)
        xg = xn[:, sl]
        s = jnp.concatenate([hn[:, sl], xg], axis=0)
        k = 1
        while k < w:
            s = s + pltpu.roll(s, k, axis=0)
            k *= 2
        cnt = jnp.minimum(pos + 1, w).astype(F32)
        pooled = s[POOL_HALO:] / cnt - xg
        mixed = jnp.dot(pooled.astype(BF16), w_ref[gi], preferred_element_type=F32)
        o_ref[:, sl] = x[:, sl] + mixed * sc_ref[:, sl]


def _pool_layer(x, norm_g, w_bf16, scale, *, ts=256):
    s_len, d = x.shape
    ng, gc, _ = w_bf16.shape
    assert ng == len(POOL_WINDOWS) and ng * gc == d and s_len % ts == 0 and ts % POOL_HALO == 0
    per = ts // POOL_HALO
    return pl.pallas_call(
        functools.partial(_pool_kernel, ts=ts, gc=gc),
        grid=(s_len // ts,),
        in_specs=[
            pl.BlockSpec((ts, d), lambda i: (i, 0)),
            pl.BlockSpec((POOL_HALO, d), lambda i: (jnp.maximum(i * per - 1, 0), 0)),
            _const_spec((1, d)),
            _const_spec((ng, gc, gc)),
            _const_spec((1, d)),
        ],
        out_specs=pl.BlockSpec((ts, d), lambda i: (i, 0)),
        out_shape=jax.ShapeDtypeStruct((s_len, d), F32),
        compiler_params=_params("parallel"),
        name="pool_layer",
    )(x, x, norm_g.reshape(1, d), w_bf16, scale.reshape(1, d))


def _ffn_kernel(h_ref, g_ref, wg_ref, wv_ref, cwg_ref, cwv_ref, bg_ref, bv_ref, wo_ref, fg_ref,
                o_ref, xn_s, a0_s, a1_s, ug0_s, ug1_s, uv0_s, uv1_s, cg_s, cv_s, *, tm, tn, nf, final):
    i = pl.program_id(0)
    f = pl.program_id(1)
    acts = (a0_s, a1_s)
    ugs = (ug0_s, ug1_s)
    uvs = (uv0_s, uv1_s)

    c = jnp.clip(f - 1, 0, nf - 1)

    @pl.when((i == 0) & (f >= 1) & (f <= nf))
    def _():
        cg_s[c] = jnp.zeros(cg_s.shape[1:], F32)
        cv_s[c] = jnp.zeros(cv_s.shape[1:], F32)

    n_down = o_ref.shape[1] // tn
    rows = tm // CONV_PIECES

    def up_piece(p, which, half):
        u_s, w_ref = ((ugs[p], wg_ref), (uvs[p], wv_ref))[which]
        r = slice(half * (tm // 2), (half + 1) * (tm // 2))
        u_s[r] = jnp.dot(xn_s[r], w_ref[...], preferred_element_type=F32)

    def conv_rows(u_s, cw_ref, b_ref, c_s, r0):
        prev = c_s[c] if r0 == 0 else u_s[r0 - CONV_HALO:r0]
        u = u_s[r0:r0 + rows]
        cat = jnp.concatenate([prev, u], axis=0)
        u1 = pltpu.roll(cat, 1, axis=0)[CONV_HALO:]
        u2 = pltpu.roll(cat, 2, axis=0)[CONV_HALO:]
        cw = cw_ref[...]
        return cw[0:1] * u2 + cw[1:2] * u1 + cw[2:3] * u + b_ref[...]

    def conv_piece(p, k):
        r0 = k * rows
        gate = conv_rows(ugs[p], cwg_ref, bg_ref, cg_s, r0)
        val = conv_rows(uvs[p], cwv_ref, bv_ref, cv_s, r0)
        acts[p][r0:r0 + rows] = (gate * (1.0 / (1.0 + jnp.exp(-gate))) * val).astype(BF16)
        if k == CONV_PIECES - 1:
            cg_s[c] = ugs[p][tm - CONV_HALO:]
            cv_s[c] = uvs[p][tm - CONV_HALO:]

    def down_piece(p, n):
        sl = slice(n * tn, (n + 1) * tn)
        o_ref[:, sl] += jnp.dot(acts[p][...], wo_ref[:, sl], preferred_element_type=F32)

    def run(p_up=None, p_conv=None, p_down=None):
        mxu, vpu = [], []
        if p_up is not None:
            mxu += [functools.partial(up_piece, p_up, w, half) for half in range(2) for w in range(2)]
        if p_down is not None:
            mxu += [functools.partial(down_piece, p_down, n) for n in range(n_down)]
        if p_conv is not None:
            vpu += [functools.partial(conv_piece, p_conv, k) for k in range(CONV_PIECES)]
        for idx in range(max(len(mxu), len(vpu))):
            if idx < len(mxu):
                mxu[idx]()
            if idx < len(vpu):
                vpu[idx]()

    @pl.when(f == 0)
    def _():
        rn = tm // NORM_PIECES

        def norm_rows(k):
            r = slice(k * rn, (k + 1) * rn)
            h = h_ref[r]
            xn_s[r] = _rms(h, g_ref[...]).astype(BF16)
            o_ref[r] = h

        def up_rows(k):
            r = slice(k * rn, (k + 1) * rn)
            ugs[0][r] = jnp.dot(xn_s[r], wg_ref[...], preferred_element_type=F32)
            uvs[0][r] = jnp.dot(xn_s[r], wv_ref[...], preferred_element_type=F32)

        norm_rows(0)
        for k in range(NORM_PIECES):
            if k + 1 < NORM_PIECES:
                norm_rows(k + 1)
            up_rows(k)

    @pl.when(f == 1)
    def _():
        run(p_up=1, p_conv=0)

    for p in range(2):
        @pl.when((f >= 2) & (f < nf) & (f % 2 == p))
        def _():
            run(p_up=p, p_conv=1 - p, p_down=p)

    @pl.when(f == nf)
    def _():
        run(p_conv=(nf - 1) % 2, p_down=nf % 2)

    @pl.when(f == nf + 1)
    def _():
        run(p_down=(nf - 1) % 2)
        if final:
            o_ref[...] = _rms(o_ref[...], fg_ref[...])


def _ffn_layer(h, norm_g, w_in_bf16, conv_w, conv_b, w_out_bf16, final_g, *, final, tm=512, tf=256,
               tn=256):
    s_len, d = h.shape
    dff = w_out_bf16.shape[0]
    tn = min(tn, d)
    assert s_len % tm == 0 and dff % tf == 0 and d % tn == 0 and w_in_bf16.shape == (d, 2 * dff)
    nf = dff // tf
    assert nf >= 2
    conv_b2 = conv_b.reshape(1, 2 * dff)
    up = lambda f: jnp.minimum(f, nf - 1)
    cv = lambda f: jnp.clip(f - 1, 0, nf - 1)
    dn = lambda f: jnp.clip(f - 2, 0, nf - 1)
    return pl.pallas_call(
        functools.partial(_ffn_kernel, tm=tm, tn=tn, nf=nf, final=final),
        grid=(s_len // tm, nf + 2),
        in_specs=[
            pl.BlockSpec((tm, d), lambda i, f: (i, 0), pipeline_mode=pl.Buffered(1)),
            _const_spec((1, d)),
            pl.BlockSpec((d, tf), lambda i, f: (0, up(f))),
            pl.BlockSpec((d, tf), lambda i, f: (0, nf + up(f))),
            pl.BlockSpec((CONV_WIDTH, tf), lambda i, f: (0, cv(f))),
            pl.BlockSpec((CONV_WIDTH, tf), lambda i, f: (0, nf + cv(f))),
            pl.BlockSpec((1, tf), lambda i, f: (0, cv(f))),
            pl.BlockSpec((1, tf), lambda i, f: (0, nf + cv(f))),
            pl.BlockSpec((tf, d), lambda i, f: (dn(f), 0)),
            _const_spec((1, d)),
        ],
        out_specs=pl.BlockSpec((tm, d), lambda i, f: (i, 0)),
        out_shape=jax.ShapeDtypeStruct((s_len, d), F32),
        scratch_shapes=[
            pltpu.VMEM((tm, d), BF16),
            pltpu.VMEM((tm, tf), BF16),
            pltpu.VMEM((tm, tf), BF16),
            pltpu.VMEM((tm, tf), F32),
            pltpu.VMEM((tm, tf), F32),
            pltpu.VMEM((tm, tf), F32),
            pltpu.VMEM((tm, tf), F32),
            pltpu.VMEM((nf, CONV_HALO, tf), F32),
            pltpu.VMEM((nf, CONV_HALO, tf), F32),
        ],
        compiler_params=_params("arbitrary", "arbitrary"),
        name="conv_ffn",
    )(h, norm_g.reshape(1, d), w_in_bf16, w_in_bf16, conv_w, conv_w, conv_b2, conv_b2, w_out_bf16,
      final_g.reshape(1, d))


def _kv_kernel(h_ref, g_ref, wd_ref, lat_ref, cos_ref, sin_ref, wuk_ref, wuvt_ref,
               k_ref, vt_ref, c_s, ct_s, kr_s, *, lora, rope, nope, vd, hg):
    gidx = pl.program_id(1)

    @pl.when(gidx == 0)
    def _():
        hn = _rms(h_ref[...], g_ref[...]).astype(BF16)
        ckv = jnp.dot(hn, wd_ref[...], preferred_element_type=F32)
        c = _rms(ckv[:, :lora], lat_ref[...])
        c_s[...] = c.astype(BF16)
        ct_s[...] = c.T.astype(BF16)
        kr = ckv[:, lora:lora + rope] * cos_ref[...] + ckv[:, lora + rope:] * sin_ref[...]
        kr_s[...] = kr.astype(BF16)

    wuk = wuk_ref[:, pl.ds(pl.multiple_of(gidx * (hg * nope), hg * nope), hg * nope)]
    wuvt = wuvt_ref[pl.ds(pl.multiple_of(gidx * (hg * vd), hg * vd), hg * vd), :]
    kn = jnp.dot(c_s[...], wuk, preferred_element_type=F32)
    vt = jnp.dot(wuvt, ct_s[...], preferred_element_type=F32)
    tm = kn.shape[0]
    hd = nope + rope
    k_pad = k_ref.shape[2] - hd
    ones_rows = vt_ref.shape[2] - vd
    for hh in range(hg):
        k_ref[hh, :, 0:nope] = kn[:, hh * nope:(hh + 1) * nope].astype(BF16)
        k_ref[hh, :, nope:hd] = kr_s[...]
        if k_pad:
            k_ref[hh, :, hd:hd + k_pad] = jnp.zeros((tm, k_pad), BF16)
        vt_ref[hh, 0, 0:vd] = vt[hh * vd:(hh + 1) * vd].astype(BF16)
        vt_ref[hh, 0, vd:vd + ones_rows] = jnp.ones((ones_rows, tm), BF16)


def _kv_proj(h, norm_g, wd_ext_bf16, lat_g, cos2, sin2, wuk_bf16, wuvt_bf16, *, n_heads, nope, rope,
             hd_pad, ones_rows, tile, tm=256, hg=32):
    s_len, d = h.shape
    lora = wuk_bf16.shape[0]
    vd = wuvt_bf16.shape[0] // n_heads
    hg = min(hg, n_heads)
    tm = min(tm, tile)
    per = tile // tm
    assert s_len % tile == 0 and tile % tm == 0 and n_heads % hg == 0
    assert wd_ext_bf16.shape == (d, lora + 2 * rope)
    return pl.pallas_call(
        functools.partial(_kv_kernel, lora=lora, rope=rope, nope=nope, vd=vd, hg=hg),
        grid=(s_len // tm, n_heads // hg),
        in_specs=[
            pl.BlockSpec((tm, d), lambda i, g: (i, 0)),
            _const_spec((1, d)),
            _const_spec((d, lora + 2 * rope)),
            _const_spec((1, lora)),
            pl.BlockSpec((tm, rope), lambda i, g: (i, 0)),
            pl.BlockSpec((tm, rope), lambda i, g: (i, 0)),
            _const_spec((lora, n_heads * nope)),
            _const_spec((n_heads * vd, lora)),
        ],
        out_specs=[
            pl.BlockSpec((hg, tm, hd_pad), lambda i, g: (g, i, 0)),
            pl.BlockSpec((hg, 1, vd + ones_rows, tm), lambda i, g: (g, i // per, 0, i % per)),
        ],
        out_shape=[
            jax.ShapeDtypeStruct((n_heads, s_len, hd_pad), BF16),
            jax.ShapeDtypeStruct((n_heads, s_len // tile, vd + ones_rows, tile), BF16),
        ],
        scratch_shapes=[
            pltpu.VMEM((tm, lora), BF16),
            pltpu.VMEM((lora, tm), BF16),
            pltpu.VMEM((tm, rope), BF16),
        ],
        compiler_params=_params("parallel", "arbitrary"),
        name="kv_proj",
    )(h, norm_g.reshape(1, d), wd_ext_bf16, lat_g.reshape(1, lora), cos2, sin2, wuk_bf16, wuvt_bf16)


def _q_kernel(h_ref, g_ref, wdt_ref, lat_ref, cos_ref, sin_ref, wut_ref, qt_ref, cqt_s,
              *, nope, rope, hg, q_scale):
    gidx = pl.program_id(1)

    @pl.when(gidx == 0)
    def _():
        xn = _rms(h_ref[...], g_ref[...]).astype(BF16)
        cqt = lax.dot_general(wdt_ref[...], xn, (((1,), (1,)), ((), ())),
                              preferred_element_type=F32)
        ms = jnp.mean(cqt * cqt, axis=0, keepdims=True)
        cqt_s[...] = (cqt * lax.rsqrt(ms + EPS) * lat_ref[...]).astype(BF16)

    rows = hg * (nope + rope)
    wut = wut_ref[pl.ds(pl.multiple_of(gidx * rows, rows), rows), :]
    qt = jnp.dot(wut, cqt_s[...], preferred_element_type=F32) * q_scale
    cos_t = cos_ref[...]
    sin_t = sin_ref[...]
    hd = nope + rope
    half = rope // 2
    pad = qt_ref.shape[2] - hd
    for hh in range(hg):
        b = hh * hd
        qt_ref[hh, 0, 0:nope] = qt[b:b + nope].astype(BF16)
        x1 = qt[b + nope:b + nope + half]
        x2 = qt[b + nope + half:b + hd]
        qt_ref[hh, 0, nope:nope + half] = (x1 * cos_t - x2 * sin_t).astype(BF16)
        qt_ref[hh, 0, nope + half:hd] = (x2 * cos_t + x1 * sin_t).astype(BF16)
        if pad:
            qt_ref[hh, 0, hd:hd + pad] = jnp.zeros((pad, qt.shape[1]), BF16)


def _q_proj(h, norm_g, wdqt_bf16, lat_g, cos_t, sin_t, wuqt_bf16, *, n_heads, nope, rope, hd_pad,
            q_scale, tile, tm=512, hg=16):
    s_len, d = h.shape
    qlora = wdqt_bf16.shape[0]
    hd = nope + rope
    hg = min(hg, n_heads)
    tm = min(tm, tile)
    per = tile // tm
    assert s_len % tile == 0 and tile % tm == 0 and n_heads % hg == 0
    assert wuqt_bf16.shape == (n_heads * hd, qlora)
    return pl.pallas_call(
        functools.partial(_q_kernel, nope=nope, rope=rope, hg=hg, q_scale=q_scale),
        grid=(s_len // tm, n_heads // hg),
        in_specs=[
            pl.BlockSpec((tm, d), lambda i, g: (i, 0)),
            _const_spec((1, d)),
            _const_spec((qlora, d)),
            _const_spec((qlora, 1)),
            pl.BlockSpec((rope // 2, tm), lambda i, g: (0, i)),
            pl.BlockSpec((rope // 2, tm), lambda i, g: (0, i)),
            _const_spec((n_heads * hd, qlora)),
        ],
        out_specs=pl.BlockSpec((hg, 1, hd_pad, tm), lambda i, g: (g, i // per, 0, i % per)),
        out_shape=jax.ShapeDtypeStruct((n_heads, s_len // tile, hd_pad, tile), BF16),
        scratch_shapes=[pltpu.VMEM((qlora, tm), BF16)],
        compiler_params=_params("parallel", "arbitrary"),
        name="q_proj",
    )(h, norm_g.reshape(1, d), wdqt_bf16, lat_g.reshape(qlora, 1), cos_t, sin_t, wuqt_bf16)


def _attn_kernel(qt_ref, k_ref, vt_ref, *rest, tq, tk, cb, vd, n_cast):
    w_f32 = rest[:n_cast]
    o_ref = rest[n_cast]
    w_bf16 = rest[n_cast + 1:2 * n_cast + 1]
    s0_ref, s1_ref, cmax0_ref, cmax1_ref, acc_ref, m_ref = rest[2 * n_cast + 1:]
    cmax_buf = (cmax0_ref, cmax1_ref)

    i = pl.program_id(1)
    tile = qt_ref.shape[3]
    r = tq // tk
    s_buf = (s0_ref, s1_ref)

    def scores(j, slot, c0, c1, rows=tk):
        kblk = k_ref[0, pl.ds(pl.multiple_of(j * tk, tk), rows), :]
        qt = qt_ref[0, c0 // tile, :, c0 % tile:c0 % tile + (c1 - c0)]
        s = jnp.dot(kblk, qt, preferred_element_type=F32)
        s_buf[slot][0:rows, c0:c1] = s
        cmax_buf[slot][:, c0:c1] = jnp.max(s, axis=0, keepdims=True)

    def visible_rows(c1, key_offset):
        return min(tk, c1 - key_offset)

    def weights_values(j, slot, c0, c1, key_offset):
        rows = tk if key_offset is None else visible_rows(c1, key_offset)
        s = s_buf[slot][0:rows, c0:c1]
        if key_offset is not None and key_offset + rows - 1 > c0:
            kpos = lax.broadcasted_iota(jnp.int32, s.shape, 0) + key_offset
            qpos = lax.broadcasted_iota(jnp.int32, s.shape, 1) + c0
            s = jnp.where(kpos <= qpos, s, MASK_VALUE)
            cmax = jnp.max(s, axis=0, keepdims=True)
        else:
            cmax = cmax_buf[slot][:, c0:c1]
        m = m_ref[:, c0:c1]
        m_new = jnp.maximum(m, cmax)
        alpha = jnp.exp2(m - m_new)
        p = jnp.exp2(s - m_new)
        m_ref[:, c0:c1] = m_new
        vt = vt_ref[0, j, :, 0:rows]
        acc_ref[:, c0:c1] = alpha * acc_ref[:, c0:c1] + jnp.dot(vt, p.astype(BF16),
                                                               preferred_element_type=F32)

    def col_blocks(lo):
        return [(c0, min(c0 + cb, tq)) for c0 in range(lo, tq, cb)]

    def stage(j, slot, lo=0, diagonal=False, next_lo=0):
        cur = col_blocks(lo)
        nxt = col_blocks(next_lo) if next_lo is not None else []
        for idx in range(max(len(cur), len(nxt))):
            if idx < len(nxt):
                scores(j + 1, 1 - slot, *nxt[idx],
                       rows=visible_rows(nxt[idx][1], next_lo) if diagonal else tk)
            if idx < len(cur):
                weights_values(j, slot, *cur[idx], key_offset=lo if diagonal else None)

    m_ref[...] = jnp.full(m_ref.shape, MASK_VALUE, F32)
    acc_ref[...] = jnp.zeros(acc_ref.shape, F32)
    for c0, c1 in col_blocks(0):
        scores(0, 0, c0, c1)

    def group(g, carry):
        for c in range(r):
            stage(g * r + c, c % 2)
        return carry

    lax.fori_loop(0, i, group, 0)
    for c in range(r):
        stage(i * r + c, c % 2, lo=c * tk, diagonal=True, next_lo=(c + 1) * tk if c < r - 1 else None)
        cols = slice(c * tk, (c + 1) * tk)
        o_ref[cols, :] = (acc_ref[0:vd, cols] / acc_ref[vd:vd + 1, cols]).T.astype(o_ref.dtype)
    for src, dst in zip(w_f32, w_bf16):
        dst[...] = src[0].astype(BF16)


def _cast_block_rows(n_rows, n_steps):
    for rb in range(BF16_SUBLANES, n_rows + 1, BF16_SUBLANES):
        if n_rows % rb == 0 and n_rows // rb <= n_steps:
            return rb
    return None


def _attention(qt, k, vt, stacked_weights, *, vd, tq=2048, tk=512, cb=512):
    n_heads, n_tiles, hd, tile = qt.shape
    s_len = n_tiles * tile
    vda = vt.shape[2]
    tq = min(tq, s_len)
    cb = min(cb, tq)
    assert k.shape == (n_heads, s_len, hd) and vt.shape == (n_heads, n_tiles, vda, tile) and vda > vd
    assert s_len % tq == 0 and tq % (2 * tk) == 0 and tk == tile and tile % cb == 0
    nq = s_len // tq
    cast_in, cast_out, cast_shapes = [], [], []
    for w, layer in stacked_weights:
        _, n_rows, n_cols = w.shape
        rb = _cast_block_rows(n_rows, n_heads * nq)
        assert rb is not None
        blk = lambda h, i, last=n_rows // rb - 1: jnp.minimum(h * nq + i, last)
        cast_in.append(pl.BlockSpec((1, rb, n_cols), lambda h, i, blk=blk, layer=layer: (layer, blk(h, i), 0)))
        cast_out.append(pl.BlockSpec((rb, n_cols), lambda h, i, blk=blk: (blk(h, i), 0)))
        cast_shapes.append(jax.ShapeDtypeStruct((n_rows, n_cols), BF16))
    outs = pl.pallas_call(
        functools.partial(_attn_kernel, tq=tq, tk=tk, cb=cb, vd=vd, n_cast=len(stacked_weights)),
        grid=(n_heads, nq),
        in_specs=[
            pl.BlockSpec((1, tq // tile, hd, tile), lambda h, i: (h, i, 0, 0)),
            pl.BlockSpec((1, s_len, hd), lambda h, i: (h, 0, 0)),
            pl.BlockSpec((1, n_tiles, vda, tile), lambda h, i: (h, 0, 0, 0)),
        ] + cast_in,
        out_specs=[pl.BlockSpec((tq, vd), lambda h, i: (i, h))] + cast_out,
        out_shape=[jax.ShapeDtypeStruct((s_len, n_heads * vd), BF16)] + cast_shapes,
        scratch_shapes=[
            pltpu.VMEM((tk, tq), F32),
            pltpu.VMEM((tk, tq), F32),
            pltpu.VMEM((1, tq), F32),
            pltpu.VMEM((1, tq), F32),
            pltpu.VMEM((vda, tq), F32),
            pltpu.VMEM((1, tq), F32),
        ],
        compiler_params=_params("arbitrary", "arbitrary"),
        name="mla_attention",
    )(qt, k, vt, *[w for w, _ in stacked_weights])
    return outs[0], outs[1:]


def _oproj_kernel(a_ref, w_ref, r_ref, o_ref):
    o_ref[...] = r_ref[...] + jnp.dot(a_ref[...], w_ref[...], preferred_element_type=F32)


def _out_proj(a_bf16, w_bf16, resid, *, tm=512, tn=1024):
    s_len, kdim = a_bf16.shape
    d = w_bf16.shape[1]
    tn = min(tn, d)
    assert s_len % tm == 0 and d % tn == 0
    return pl.pallas_call(
        _oproj_kernel,
        grid=(s_len // tm, d // tn),
        in_specs=[
            pl.BlockSpec((tm, kdim), lambda i, j: (i, 0)),
            pl.BlockSpec((kdim, tn), lambda i, j: (0, j)),
            pl.BlockSpec((tm, tn), lambda i, j: (i, j)),
        ],
        out_specs=pl.BlockSpec((tm, tn), lambda i, j: (i, j)),
        out_shape=jax.ShapeDtypeStruct((s_len, d), F32),
        compiler_params=_params("parallel", "parallel"),
        name="attn_out_proj",
    )(a_bf16, w_bf16, resid)


def _trunk(x, a_norm, a_pool_w, a_scale, kv_norm, w_dkv, kv_lat_norm, w_ukv, b_norm, w_dq, q_lat_norm,
           w_uq, w_o, ffn_norm, ffn_w_in, ffn_conv_w, ffn_conv_b, ffn_w_out, final_norm,
           *, pool_ts=256, ffn_tm=512, ffn_tf=256, q_tm=512, kv_tm=256, attn_tq=2048, attn_tk=512, attn_cb=256,
           oproj_tm=512):
    batch, s_len, d = x.shape
    depth = ffn_norm.shape[0]
    n_a = a_norm.shape[0]
    n_b = b_norm.shape[0]
    assert n_a + n_b == depth and n_a >= 1
    lora, n_heads, kvd = w_ukv.shape
    hd = w_uq.shape[-1]
    rope = w_dkv.shape[1] - lora
    nope = hd - rope
    vd = kvd - nope
    half = rope // 2
    hd_pad = -(-hd // LANES) * LANES

    expo = -np.arange(0, rope, 2, dtype=np.float32) / np.float32(rope)
    inv_freq = (np.float64(ROPE_THETA) ** expo.astype(np.float64)).astype(np.float32)
    ang = np.arange(s_len, dtype=np.float32)[:, None] * inv_freq[None, :]
    cos = np.cos(ang.astype(np.float64)).astype(np.float32)
    sin = np.sin(ang.astype(np.float64)).astype(np.float32)
    cos2 = jnp.asarray(np.concatenate([cos, cos], axis=1))
    sin2 = jnp.asarray(np.concatenate([-sin, sin], axis=1))
    cos_t, sin_t = jnp.asarray(cos.T.copy()), jnp.asarray(sin.T.copy())

    bf = lambda w: w.astype(BF16)
    wd_ext = bf(jnp.concatenate([w_dkv, w_dkv[:, lora + half:], w_dkv[:, lora:lora + half]], axis=1))
    wuk = bf(w_ukv[:, :, :nope].reshape(lora, n_heads * nope))
    wuvt = bf(w_ukv[:, :, nope:].reshape(lora, n_heads * vd).T)
    final_g = final_norm

    outs = []
    ffn_bf16 = {}
    wo_bf16 = {}
    for b in range(batch):
        h = x.reshape(s_len, d) if batch == 1 else x[b]
        shared = None
        for layer in range(depth):
            last = layer == depth - 1
            if layer < n_a:
                h = _pool_layer(h, a_norm[layer], bf(a_pool_w[layer]), a_scale[layer], ts=pool_ts)
            else:
                j = layer - n_a
                qlora = w_dq.shape[2]
                qt = _q_proj(h, b_norm[j], bf(w_dq[j].T), q_lat_norm[j], cos_t, sin_t,
                             bf(w_uq[j].reshape(qlora, n_heads * hd).T),
                             n_heads=n_heads, nope=nope, rope=rope, hd_pad=hd_pad,
                             q_scale=hd ** -0.5 * math.log2(math.e), tile=attn_tk, tm=q_tm)
                first = layer not in ffn_bf16
                to_cast = ((ffn_w_in, layer), (ffn_w_out, layer), (w_o, j)) if first else ()
                o, cast = _attention(qt, shared[0], shared[1], to_cast, vd=vd, tq=attn_tq, tk=attn_tk,
                                     cb=attn_cb)
                if first:
                    ffn_bf16[layer] = cast[:2]
                    wo_bf16[j] = cast[2]
                h = _out_proj(o, wo_bf16[j], h, tm=oproj_tm)
            if layer not in ffn_bf16:
                ffn_bf16[layer] = (bf(ffn_w_in[layer]), bf(ffn_w_out[layer]))
            h = _ffn_layer(h, ffn_norm[layer], ffn_bf16[layer][0], ffn_conv_w[layer], ffn_conv_b[layer],
                           ffn_bf16[layer][1], final_g, final=last, tm=ffn_tm, tf=ffn_tf)
            if layer == n_a - 1:
                shared = _kv_proj(h, kv_norm, wd_ext, kv_lat_norm, cos2, sin2, wuk, wuvt,
                                  n_heads=n_heads, nope=nope, rope=rope, hd_pad=hd_pad,
                                  ones_rows=BF16_SUBLANES, tile=attn_tk, tm=kv_tm)
        outs.append(h)
    return outs[0].reshape(1, s_len, d) if batch == 1 else jnp.stack(outs, axis=0)


def kernel(x, a_norm, a_pool_w, a_scale, kv_norm, w_dkv, kv_lat_norm, w_ukv, b_norm, w_dq, q_lat_norm,
           w_uq, w_o, ffn_norm, ffn_w_in, ffn_conv_w, ffn_conv_b, ffn_w_out, final_norm):
    return _trunk(x, a_norm, a_pool_w, a_scale, kv_norm, w_dkv, kv_lat_norm, w_ukv, b_norm, w_dq,
                  q_lat_norm, w_uq, w_o, ffn_norm, ffn_w_in, ffn_conv_w, ffn_conv_b, ffn_w_out,
                  final_norm)
```
